```python
import math
import jax
import jax.numpy as jnp
from jax import lax
import numpy as np

D_MODEL = 2048
BATCH = 4
SEQ = 8192
DEPTH = 4

GRID_W = 64
CTX_LEN = 256
N_BRANCH = 4
D_BRANCH = D_MODEL // N_BRANCH
CHUNK = 128
GMLP_GROUPS = 4
GMLP_GW = D_BRANCH // GMLP_GROUPS
CONV_W = 31
HEAD_DIM = 64
N_Q_HEADS = D_BRANCH // HEAD_DIM
N_KV_HEADS = 2
Q_PER_KV = N_Q_HEADS // N_KV_HEADS
WINDOW = 128
BLOCK = 128
ROPE_BASE = 10000.0
S5_GW = 16
S5_GROUPS = D_BRANCH // S5_GW
S5_STATE = 64
D_FF = 4 * D_MODEL
N_MOD = 6
EPS = 1e-6
NEG_INF = -1e30
COLS_A = 2 * D_BRANCH
COLS_B = 2 * D_BRANCH
COLS_Q = N_Q_HEADS * HEAD_DIM
COLS_KV = N_KV_HEADS * HEAD_DIM
COLS_D = D_BRANCH
IN_COLS = COLS_A + COLS_B + COLS_Q + 2 * COLS_KV + COLS_D

kernel_name = 'hybrid_parallel_gmlp_conformer_swa_s5_dit'


def rms_norm(x, g):
    xf = x.astype(jnp.float32)
    y = xf * lax.rsqrt(jnp.mean(xf * xf, axis=-1, keepdims=True) + EPS)
    return (y * g.astype(jnp.float32)).astype(x.dtype)


def layer_norm(x, g, b):
    xf = x.astype(jnp.float32)
    xc = xf - jnp.mean(xf, axis=-1, keepdims=True)
    var = jnp.mean(xc * xc, axis=-1, keepdims=True)
    return (xc * lax.rsqrt(var + EPS) * g.astype(jnp.float32) + b.astype(jnp.float32)).astype(x.dtype)


def rope_1d(x, pos):
    d = x.shape[-1]
    inv = ROPE_BASE ** (-jnp.arange(0, d, 2, dtype=jnp.float32) / d)
    ang = pos.astype(jnp.float32)[:, None] * inv[None, :]
    cos = jnp.cos(ang)[:, None, :]
    sin = jnp.sin(ang)[:, None, :]
    xf = x.astype(jnp.float32)
    x1, x2 = xf[..., : d // 2], xf[..., d // 2:]
    return jnp.concatenate([x1 * cos - x2 * sin, x1 * sin + x2 * cos], axis=-1).astype(x.dtype)


def axial_rope(x, row, col):
    half = HEAD_DIM // 2
    return jnp.concatenate([rope_1d(x[..., :half], row), rope_1d(x[..., half:], col)], axis=-1)


def split_in(z):
    parts = []
    start = 0
    for width in (COLS_A, COLS_B, COLS_Q, COLS_KV, COLS_KV, COLS_D):
        parts.append(z[..., start:start + width])
        start += width
    return parts


def gmlp_chunk_mix(za, ln_g, ln_b, w_s, b_s):
    za = jax.nn.gelu(za)
    u, v = jnp.split(za, 2, axis=-1)
    v = layer_norm(v, ln_g, ln_b)
    B, L, _ = v.shape
    vb = v.reshape(B, L // CHUNK, CHUNK, GMLP_GROUPS, GMLP_GW)
    mixed = jnp.einsum('gpq,bnqgc->bnpgc', w_s, vb) + b_s.T[None, None, :, :, None]
    return u * mixed.reshape(B, L, D_BRANCH)


def conformer_conv(zb, w_dw, b_dw, ln_g, ln_b):
    a, g = jnp.split(zb, 2, axis=-1)
    y = a * jax.nn.sigmoid(g)
    y = lax.conv_general_dilated(
        y, w_dw[:, None, :], window_strides=(1,), padding=[(CONV_W // 2, CONV_W // 2)],
        dimension_numbers=('NWC', 'WIO', 'NWC'), feature_group_count=D_BRANCH) + b_dw
    return jax.nn.silu(layer_norm(y, ln_g, ln_b))


def windowed_attention(q, k, v, kc, vc, sink):
    B, S = q.shape[:2]
    nb = S // BLOCK
    scale = HEAD_DIM ** -0.5
    qb = q.reshape(B, nb, BLOCK, N_KV_HEADS, Q_PER_KV, HEAD_DIM)

    def band(t):
        tp = jnp.pad(t, ((0, 0), (BLOCK, BLOCK), (0, 0), (0, 0)))
        tp = tp.reshape(B, nb + 2, BLOCK, N_KV_HEADS, HEAD_DIM)
        return jnp.concatenate([tp[:, :-2], tp[:, 1:-1], tp[:, 2:]], axis=2)

    kb, vb = band(k), band(v)
    blk = jnp.arange(nb)[:, None, None]
    qpos = blk * BLOCK + jnp.arange(BLOCK)[None, :, None]
    kpos = (blk - 1) * BLOCK + jnp.arange(3 * BLOCK)[None, None, :]
    allowed = (jnp.abs(qpos - kpos) <= WINDOW) & (kpos >= 0) & (kpos < S)
    s_loc = jnp.einsum('bnqhgd,bnkhd->bnhgqk', qb, kb, preferred_element_type=jnp.float32) * scale
    s_loc = jnp.where(allowed[None, :, None, None], s_loc, NEG_INF)
    s_ctx = jnp.einsum('bnqhgd,bchd->bnhgqc', qb, kc, preferred_element_type=jnp.float32) * scale
    sink_col = jnp.broadcast_to(sink.astype(jnp.float32).reshape(1, 1, N_KV_HEADS, Q_PER_KV, 1, 1),
                                s_loc.shape[:-1] + (1,))
    p = jax.nn.softmax(jnp.concatenate([s_loc, s_ctx, sink_col], axis=-1), axis=-1)
    n_loc = 3 * BLOCK
    n_ctx = kc.shape[1]
    p_loc = p[..., :n_loc].astype(v.dtype)
    p_ctx = p[..., n_loc:n_loc + n_ctx].astype(v.dtype)
    o = (jnp.einsum('bnhgqk,bnkhd->bnqhgd', p_loc, vb)
         + jnp.einsum('bnhgqc,bchd->bnqhgd', p_ctx, vc))
    return o.reshape(B, S, N_Q_HEADS * HEAD_DIM)


def context_attention(qc, kc, vc, sink):
    B, C = qc.shape[:2]
    scale = HEAD_DIM ** -0.5
    qg = qc.reshape(B, C, N_KV_HEADS, Q_PER_KV, HEAD_DIM)
    s = jnp.einsum('bqhgd,bkhd->bhgqk', qg, kc, preferred_element_type=jnp.float32) * scale
    sink_col = jnp.broadcast_to(sink.astype(jnp.float32).reshape(1, N_KV_HEADS, Q_PER_KV, 1, 1),
                                s.shape[:-1] + (1,))
    p = jax.nn.softmax(jnp.concatenate([s, sink_col], axis=-1), axis=-1)
    o = jnp.einsum('bhgqk,bkhd->bqhgd', p[..., :C].astype(vc.dtype), vc)
    return o.reshape(B, C, N_Q_HEADS * HEAD_DIM)


def s5_discretise(a_re, a_im, log_step, b_re, b_im):
    lam = lax.complex(a_re.astype(jnp.float32), a_im.astype(jnp.float32))
    dt = jnp.exp(log_step.astype(jnp.float32))[:, None]
    log_lbar = lam * dt
    lbar = jnp.exp(log_lbar)
    b = lax.complex(b_re.astype(jnp.float32), b_im.astype(jnp.float32))
    bbar = ((lbar - 1.0) / lam)[..., None] * b
    return log_lbar, lbar, bbar


def _ssm_combine(left, right):
    a1, b1 = left
    a2, b2 = right
    return a1 * a2, a2 * b1 + b2


def s5_scan(u, disc, s0):
    log_lbar, lbar, bbar = disc
    bu = jnp.einsum('blgc,gpc->blgp', u.astype(jnp.float32).astype(jnp.complex64), bbar)
    a = jnp.broadcast_to(lbar, bu.shape)
    _, s = lax.associative_scan(_ssm_combine, (a, bu), axis=1)
    if s0 is not None:
        steps = jnp.arange(1, u.shape[1] + 1, dtype=jnp.float32)
        s = s + jnp.exp(log_lbar[None] * steps[:, None, None])[None] * s0[:, None]
    return s


def s5_readout(u, s_f, s_b, c_re, c_im, d_skip, w_glu):
    cf = lax.complex(c_re[0].astype(jnp.float32), c_im[0].astype(jnp.float32))
    cb = lax.complex(c_re[1].astype(jnp.float32), c_im[1].astype(jnp.float32))
    y = (jnp.einsum('gcp,blgp->blgc', cf, s_f) + jnp.einsum('gcp,blgp->blgc', cb, s_b)).real
    B, L = u.shape[:2]
    y = y.reshape(B, L, D_BRANCH) + d_skip.astype(jnp.float32) * u.reshape(B, L, D_BRANCH).astype(jnp.float32)
    y = jax.nn.gelu(y).astype(u.dtype)
    a, g = jnp.split(y @ w_glu, 2, axis=-1)
    return a * jax.nn.sigmoid(g)


def merge_branches(h, branches, w_br, w_gate, b_gate, w_out):
    merged = None
    for kb, br in enumerate(branches):
        term = jax.nn.sigmoid(h @ w_gate[kb] + b_gate[kb]) * (br @ w_br[kb])
        merged = term if merged is None else merged + term
    return merged @ w_out


def sq_relu_mlp(h, w1, w2):
    return jnp.square(jax.nn.relu(h @ w1)) @ w2


def setup_inputs(seed: int = 0) -> dict:
    key = jax.random.key(seed)
    keys = jax.random.split(key, 40)

    def nrm(i, shape, s):
        return jax.random.normal(keys[i], shape, jnp.float32) * s

    L, D, G, P = DEPTH, D_MODEL, S5_GROUPS, S5_STATE
    n_idx = jnp.arange(P, dtype=jnp.float32)
    return {
        'x': nrm(0, (BATCH, SEQ, D), 1.0),
        'c': nrm(1, (BATCH, D), 1.0),
        'ctx': nrm(2, (BATCH, CTX_LEN, D), 1.0),
        'c_ctx': nrm(3, (D,), 1.0),
        'w_mod': nrm(4, (L, D, N_MOD * D), 0.5 * D ** -0.5),
        'b_mod': nrm(5, (L, N_MOD * D), 0.02),
        'norm1_g': 1.0 + nrm(6, (L, D), 0.02),
        'norm2_g': 1.0 + nrm(7, (L, D), 0.02),
        'w_in': nrm(8, (L, D, IN_COLS), D ** -0.5),
        'gmlp_ln_g': 1.0 + nrm(9, (L, D_BRANCH), 0.02),
        'gmlp_ln_b': nrm(10, (L, D_BRANCH), 0.02),
        'gmlp_ws': nrm(11, (L, GMLP_GROUPS, CHUNK, CHUNK), CHUNK ** -0.5),
        'gmlp_bs': 1.0 + nrm(12, (L, GMLP_GROUPS, CHUNK), 0.02),
        'conv_w': nrm(13, (L, CONV_W, D_BRANCH), CONV_W ** -0.5),
        'conv_b': nrm(14, (L, D_BRANCH), 0.02),
        'conv_ln_g': 1.0 + nrm(15, (L, D_BRANCH), 0.02),
        'conv_ln_b': nrm(16, (L, D_BRANCH), 0.02),
        'attn_sink': nrm(17, (L, N_Q_HEADS), 0.5),
        's5_a_re': -0.5 + nrm(18, (L, 2, G, P), 0.01),
        's5_a_im': math.pi * n_idx + nrm(19, (L, 2, G, P), 0.01),
        's5_log_step': jax.random.uniform(keys[20], (L, 2, G), jnp.float32, math.log(1e-3), math.log(1e-1)),
        's5_b_re': nrm(21, (L, G, P, S5_GW), (2 * S5_GW) ** -0.5),
        's5_b_im': nrm(22, (L, G, P, S5_GW), (2 * S5_GW) ** -0.5),
        's5_c_re': nrm(23, (L, 2, G, S5_GW, P), 0.25),
        's5_c_im': nrm(24, (L, 2, G, S5_GW, P), 0.25),
        's5_d': nrm(25, (L, D_BRANCH), 0.5),
        's5_w_glu': nrm(26, (L, D_BRANCH, 2 * D_BRANCH), D_BRANCH ** -0.5),
        'w_branch': nrm(27, (L, N_BRANCH, D_BRANCH, D), D_BRANCH ** -0.5),
        'w_gate': nrm(28, (L, N_BRANCH, D, D), D ** -0.5),
        'b_gate': nrm(29, (L, N_BRANCH, D), 0.02),
        'w_out': nrm(30, (L, D, D), D ** -0.5),
        'w_ff1': nrm(31, (L, D, D_FF), D ** -0.5),
        'w_ff2': nrm(32, (L, D_FF, D), D_FF ** -0.5),
        'final_g': 1.0 + nrm(33, (D,), 0.02),
    }


def reference(x, c, ctx, c_ctx, w_mod, b_mod, norm1_g, norm2_g, w_in,
              gmlp_ln_g, gmlp_ln_b, gmlp_ws, gmlp_bs,
              conv_w, conv_b, conv_ln_g, conv_ln_b,
              attn_sink,
              s5_a_re, s5_a_im, s5_log_step, s5_b_re, s5_b_im, s5_c_re, s5_c_im, s5_d, s5_w_glu,
              w_branch, w_gate, b_gate, w_out, w_ff1, w_ff2, final_g):
    B, S, _ = x.shape
    n_ctx = ctx.shape[1]
    rows = S // GRID_W
    row = jnp.repeat(jnp.arange(rows), GRID_W)
    col = jnp.tile(jnp.arange(GRID_W), rows)
    cond_x = jax.nn.silu(c)
    cond_c = jax.nn.silu(c_ctx)
    for l in range(DEPTH):
        mod_x = (cond_x @ w_mod[l] + b_mod[l])[:, None, :]
        mod_c = cond_c @ w_mod[l] + b_mod[l]
        shift1, scale1, gate1, shift2, scale2, gate2 = jnp.split(mod_x, N_MOD, axis=-1)
        cshift1, cscale1, cgate1, cshift2, cscale2, cgate2 = jnp.split(mod_c, N_MOD, axis=-1)
        disc_f = s5_discretise(s5_a_re[l, 0], s5_a_im[l, 0], s5_log_step[l, 0], s5_b_re[l], s5_b_im[l])
        disc_b = s5_discretise(s5_a_re[l, 1], s5_a_im[l, 1], s5_log_step[l, 1], s5_b_re[l], s5_b_im[l])

        hc = rms_norm(ctx, norm1_g[l]) * (1.0 + cscale1) + cshift1
        zc_a, zc_b, qc, kc, vc, zc_d = split_in(hc @ w_in[l])
        kc = kc.reshape(B, n_ctx, N_KV_HEADS, HEAD_DIM)
        vc = vc.reshape(B, n_ctx, N_KV_HEADS, HEAD_DIM)
        uc = zc_d.reshape(B, n_ctx, S5_GROUPS, S5_GW)
        sc_f = s5_scan(uc, disc_f, None)
        sc_b_rev = s5_scan(jnp.flip(uc, 1), disc_b, None)

        h = rms_norm(x, norm1_g[l]) * (1.0 + scale1) + shift1
        z_a, z_b, q, k, v, z_d = split_in(h @ w_in[l])
        q = axial_rope(q.reshape(B, S, N_Q_HEADS, HEAD_DIM), row, col)
        k = axial_rope(k.reshape(B, S, N_KV_HEADS, HEAD_DIM), row, col)
        v = v.reshape(B, S, N_KV_HEADS, HEAD_DIM)
        u = z_d.reshape(B, S, S5_GROUPS, S5_GW)
        s_f = s5_scan(u, disc_f, sc_f[:, -1])
        s_b = jnp.flip(s5_scan(jnp.flip(u, 1), disc_b, sc_b_rev[:, -1]), 1)
        branches = (
            gmlp_chunk_mix(z_a, gmlp_ln_g[l], gmlp_ln_b[l], gmlp_ws[l], gmlp_bs[l]),
            conformer_conv(z_b, conv_w[l], conv_b[l], conv_ln_g[l], conv_ln_b[l]),
            windowed_attention(q, k, v, kc, vc, attn_sink[l]),
            s5_readout(u, s_f, s_b, s5_c_re[l], s5_c_im[l], s5_d[l], s5_w_glu[l]),
        )
        x = x + gate1 * merge_branches(h, branches, w_branch[l], w_gate[l], b_gate[l], w_out[l])
        h2 = rms_norm(x, norm2_g[l]) * (1.0 + scale2) + shift2
        x = x + gate2 * sq_relu_mlp(h2, w_ff1[l], w_ff2[l])

        if l < DEPTH - 1:
            branches_c = (
                gmlp_chunk_mix(zc_a, gmlp_ln_g[l], gmlp_ln_b[l], gmlp_ws[l], gmlp_bs[l]),
                conformer_conv(zc_b, conv_w[l], conv_b[l], conv_ln_g[l], conv_ln_b[l]),
                context_attention(qc, kc, vc, attn_sink[l]),
                s5_readout(uc, sc_f, jnp.flip(sc_b_rev, 1), s5_c_re[l], s5_c_im[l], s5_d[l], s5_w_glu[l]),
            )
            ctx = ctx + cgate1 * merge_branches(hc, branches_c, w_branch[l], w_gate[l], b_gate[l], w_out[l])
            hc2 = rms_norm(ctx, norm2_g[l]) * (1.0 + cscale2) + cshift2
            ctx = ctx + cgate2 * sq_relu_mlp(hc2, w_ff1[l], w_ff2[l])
    return rms_norm(x, final_g)
```

```python
import functools
import math

import jax
import jax.numpy as jnp
from jax import lax
from jax.experimental import pallas as pl
from jax.experimental.pallas import tpu as pltpu

D_MODEL = 2048
N_BRANCH = 4
D_BRANCH = D_MODEL // N_BRANCH
CHUNK = 128
GMLP_GROUPS = 4
CONV_W = 31
CONV_HALO = 16
HEAD_DIM = 64
N_Q_HEADS = D_BRANCH // HEAD_DIM
N_KV_HEADS = 2
Q_PER_KV = N_Q_HEADS // N_KV_HEADS
WINDOW = 128
ROPE_BASE = 10000.0
GRID_W = 64
S5_GW = 16
S5_GROUPS = D_BRANCH // S5_GW
S5_STATE = 64
S5_PAD_B = 8
D_FF = 4 * D_MODEL
N_MOD = 6
EPS = 1e-6
NEG_INF = -1e30

COL_A = 0
COL_B = 2 * D_BRANCH
COL_Q = 4 * D_BRANCH
COL_D = 5 * D_BRANCH
COL_K = 6 * D_BRANCH
COL_V = COL_K + N_KV_HEADS * HEAD_DIM
IN_COLS = COL_V + N_KV_HEADS * HEAD_DIM

VMEM_LIMIT_V7X = 56 * 1024 * 1024

F32 = jnp.float32
BF16 = jnp.bfloat16


def _cparams(*sem):
    return pltpu.CompilerParams(dimension_semantics=sem, vmem_limit_bytes=VMEM_LIMIT_V7X)


def _gelu_tanh(x):
    return 0.5 * x * (1.0 + jnp.tanh(math.sqrt(2.0 / math.pi) * (x + 0.044715 * (x * x * x))))


def _sigmoid(x):
    return 1.0 / (1.0 + jnp.exp(-x))


def _layer_norm_rows(v, g, b):
    mu = jnp.mean(v, axis=-1, keepdims=True)
    vc = v - mu
    var = jnp.mean(vc * vc, axis=-1, keepdims=True)
    return vc * lax.rsqrt(var + EPS) * g + b


def _row_tile(nc, s):
    for t in (1024, 512, 256, 128):
        if nc % t == 0 and s % t == 0:
            return t
    raise ValueError("context and sequence lengths must be multiples of 128")


def _mod_kernel(c_ref, w_ref, b_ref, o_ref):
    c = c_ref[...]
    c = (c * _sigmoid(c)).astype(BF16)
    o_ref[0] = jnp.dot(c, w_ref[0].astype(BF16), preferred_element_type=F32) + b_ref[0]


def _modulation(cond, w_mod, b_mod):
    depth, d, n = w_mod.shape
    rows = cond.shape[0]
    tn = 1024
    return pl.pallas_call(
        _mod_kernel,
        out_shape=jax.ShapeDtypeStruct((depth, rows, n), F32),
        grid=(depth, n // tn),
        in_specs=[
            pl.BlockSpec((rows, d), lambda l, j: (0, 0)),
            pl.BlockSpec((1, d, tn), lambda l, j: (l, 0, j)),
            pl.BlockSpec((1, 1, tn), lambda l, j: (l, 0, j)),
        ],
        out_specs=pl.BlockSpec((1, rows, tn), lambda l, j: (l, 0, j)),
        compiler_params=_cparams("arbitrary", "arbitrary"),
        name="modulation",
    )(cond, w_mod, b_mod.reshape(depth, 1, n))


def _inproj_kernel(x_ref, g_ref, shift_ref, scale_ref, w_ref, h_ref, z_ref):
    @pl.when(pl.program_id(1) == 0)
    def _():
        x = x_ref[...]
        y = x * lax.rsqrt(jnp.mean(x * x, axis=-1, keepdims=True) + EPS) * g_ref[...]
        h_ref[...] = (y * (1.0 + scale_ref[0]) + shift_ref[0]).astype(BF16)

    z_ref[...] = jnp.dot(h_ref[...], w_ref[...], preferred_element_type=F32)


def _in_projection(xall, norm_g, mod3, w_in, lay):
    n_tok, d = xall.shape
    tm, tn = lay["tm"], 256
    mod_row = lay["mod_row"]
    return pl.pallas_call(
        _inproj_kernel,
        out_shape=(jax.ShapeDtypeStruct((n_tok, d), BF16), jax.ShapeDtypeStruct((n_tok, IN_COLS), F32)),
        grid=(n_tok // tm, IN_COLS // tn),
        in_specs=[
            pl.BlockSpec((tm, d), lambda i, j: (i, 0)),
            pl.BlockSpec((1, d), lambda i, j: (0, 0)),
            pl.BlockSpec((1, 1, d), lambda i, j: (mod_row(i, 0), 0, 0)),
            pl.BlockSpec((1, 1, d), lambda i, j: (mod_row(i, 1), 0, 0)),
            pl.BlockSpec((d, tn), lambda i, j: (0, j)),
        ],
        out_specs=(pl.BlockSpec((tm, d), lambda i, j: (i, 0)), pl.BlockSpec((tm, tn), lambda i, j: (i, j))),
        compiler_params=_cparams("arbitrary", "arbitrary"),
        name="in_projection",
    )(xall, norm_g.reshape(1, d), mod3, mod3, w_in)


def _gmlp_kernel(za_ref, lng_ref, lnb_ref, ws_ref, bs_ref, o_ref):
    rows = za_ref.shape[0]
    gw = D_BRANCH // GMLP_GROUPS
    a = _gelu_tanh(za_ref[...])
    u = a[:, :D_BRANCH]
    v = _layer_norm_rows(a[:, D_BRANCH:], lng_ref[...], lnb_ref[...]).astype(BF16)
    for c in range(rows // CHUNK):
        r0 = c * CHUNK
        for g in range(GMLP_GROUPS):
            c0 = g * gw
            mixed = jnp.dot(ws_ref[g], v[r0:r0 + CHUNK, c0:c0 + gw], preferred_element_type=F32) + bs_ref[g]
            o_ref[r0:r0 + CHUNK, c0:c0 + gw] = (u[r0:r0 + CHUNK, c0:c0 + gw] * mixed).astype(o_ref.dtype)


def _gmlp(z, ln_g, ln_b, ws, bs, lay):
    n_tok = z.shape[0]
    rows = lay["tseq"]
    gw = D_BRANCH // GMLP_GROUPS
    bs_full = jnp.broadcast_to(bs[:, :, None], (GMLP_GROUPS, CHUNK, gw)).astype(F32)
    return pl.pallas_call(
        _gmlp_kernel,
        out_shape=jax.ShapeDtypeStruct((n_tok, D_BRANCH), BF16),
        grid=(n_tok // rows,),
        in_specs=[
            pl.BlockSpec((rows, 2 * D_BRANCH), lambda i: (i, COL_A // (2 * D_BRANCH))),
            pl.BlockSpec((1, D_BRANCH), lambda i: (0, 0)),
            pl.BlockSpec((1, D_BRANCH), lambda i: (0, 0)),
            pl.BlockSpec((GMLP_GROUPS, CHUNK, CHUNK), lambda i: (0, 0, 0)),
            pl.BlockSpec((GMLP_GROUPS, CHUNK, gw), lambda i: (0, 0, 0)),
        ],
        out_specs=pl.BlockSpec((rows, D_BRANCH), lambda i: (i, 0)),
        compiler_params=_cparams("arbitrary"),
        name="gmlp_mix",
    )(z, ln_g.reshape(1, -1), ln_b.reshape(1, -1), ws.astype(BF16), bs_full)


def _conv_kernel(zm_ref, zp_ref, zn_ref, w_ref, cb_ref, lng_ref, lnb_ref, o_ref, ybuf, *, seg_tiles):
    rows = zm_ref.shape[0]
    i = pl.program_id(0)
    nct, cpt, lpt = seg_tiles
    pos = jnp.where(i < nct, i % cpt, (i - nct) % lpt)
    seg = jnp.where(i < nct, cpt, lpt)
    keep_prev = (pos != 0).astype(F32)
    keep_next = (pos != seg - 1).astype(F32)

    def glu(ref):
        z = ref[...]
        return z[:, :D_BRANCH] * _sigmoid(z[:, D_BRANCH:])

    ybuf[0:CONV_HALO, :] = glu(zp_ref) * keep_prev
    ybuf[CONV_HALO:CONV_HALO + rows, :] = glu(zm_ref)
    ybuf[CONV_HALO + rows:, :] = glu(zn_ref) * keep_next

    sub = 32
    off = CONV_HALO - CONV_W // 2
    for r in range(rows // sub):
        acc = jnp.zeros((sub, D_BRANCH), F32)
        for k in range(CONV_W):
            acc = acc + ybuf[r * sub + off + k:r * sub + off + k + sub, :] * w_ref[k:k + 1, :]
        y = _layer_norm_rows(acc + cb_ref[...], lng_ref[...], lnb_ref[...])
        o_ref[r * sub:(r + 1) * sub, :] = (y * _sigmoid(y)).astype(o_ref.dtype)


def _conformer_conv(z, w_dw, b_dw, ln_g, ln_b, lay):
    n_tok = z.shape[0]
    rows = lay["tseq"]
    hb = rows // CONV_HALO
    n_halo_blocks = n_tok // CONV_HALO
    colb = COL_B // (2 * D_BRANCH)
    seg_tiles = (lay["nc"] // rows, lay["n_ctx"] // rows, lay["s"] // rows)
    vec = lambda a: a.reshape(1, -1)
    return pl.pallas_call(
        functools.partial(_conv_kernel, seg_tiles=seg_tiles),
        out_shape=jax.ShapeDtypeStruct((n_tok, D_BRANCH), BF16),
        grid=(n_tok // rows,),
        in_specs=[
            pl.BlockSpec((rows, 2 * D_BRANCH), lambda i: (i, colb)),
            pl.BlockSpec((CONV_HALO, 2 * D_BRANCH), lambda i: (jnp.maximum(i * hb - 1, 0), colb)),
            pl.BlockSpec((CONV_HALO, 2 * D_BRANCH), lambda i: (jnp.minimum((i + 1) * hb, n_halo_blocks - 1), colb)),
            pl.BlockSpec((CONV_W, D_BRANCH), lambda i: (0, 0)),
            pl.BlockSpec((1, D_BRANCH), lambda i: (0, 0)),
            pl.BlockSpec((1, D_BRANCH), lambda i: (0, 0)),
            pl.BlockSpec((1, D_BRANCH), lambda i: (0, 0)),
        ],
        out_specs=pl.BlockSpec((rows, D_BRANCH), lambda i: (i, 0)),
        scratch_shapes=[pltpu.VMEM((rows + 2 * CONV_HALO, D_BRANCH), F32)],
        compiler_params=_cparams("arbitrary"),
        name="conformer_conv",
    )(z, z, z, w_dw, vec(b_dw), vec(ln_g), vec(ln_b))


def _rope(x, cos, sin_a, sin_b):
    reps = x.shape[1] // cos.shape[1]
    if reps > 1:
        cos, sin_a, sin_b = (jnp.concatenate([t] * reps, axis=1) for t in (cos, sin_a, sin_b))
    n = x.shape[1]
    half = HEAD_DIM // 4
    return x * cos + pltpu.roll(x, n - half, 1) * sin_a + pltpu.roll(x, half, 1) * sin_b


def _attn_core(q, keys, vals, bias, sink_ref, o_ref):
    rows = q.shape[0]
    rid = lax.broadcasted_iota(jnp.int32, (Q_PER_KV * rows, 1), 0)
    for h in range(N_KV_HEADS):
        kh = keys[:, h * HEAD_DIM:(h + 1) * HEAD_DIM]
        vh = vals[:, h * HEAD_DIM:(h + 1) * HEAD_DIM]
        heads = [q[:, (h * Q_PER_KV + g) * HEAD_DIM:(h * Q_PER_KV + g + 1) * HEAD_DIM] for g in range(Q_PER_KV)]
        qh = jnp.concatenate(heads, axis=0).astype(BF16)
        s = lax.dot_general(qh, kh, (((1,), (1,)), ((), ())), preferred_element_type=F32)
        if bias is not None:
            s = s + jnp.concatenate([bias] * Q_PER_KV, axis=0)
        sink = jnp.full((Q_PER_KV * rows, 1), sink_ref[h * Q_PER_KV], F32)
        for g in range(1, Q_PER_KV):
            sink = jnp.where(rid >= g * rows, sink_ref[h * Q_PER_KV + g], sink)
        m = jnp.maximum(jnp.max(s, axis=-1, keepdims=True), sink)
        e = jnp.exp(s - m)
        denom = jnp.sum(e, axis=-1, keepdims=True) + jnp.exp(sink - m)
        o = jnp.dot(e.astype(BF16), vh, preferred_element_type=F32) * (1.0 / denom)
        for g in range(Q_PER_KV):
            c0 = (h * Q_PER_KV + g) * HEAD_DIM
            o_ref[:, c0:c0 + HEAD_DIM] = o[g * rows:(g + 1) * rows, :].astype(o_ref.dtype)


def _attn_latent_kernel(sink_ref, q_ref, kp_ref, kc_ref, kn_ref, vp_ref, vc_ref, vn_ref, kx_ref, vx_ref,
                        cq_ref, cp_ref, cn_ref, o_ref, *, n_blocks):
    n = pl.program_id(1)
    rows = q_ref.shape[0]
    scale = HEAD_DIM ** -0.5

    def tables(ref):
        return ref[0], ref[1], ref[2]

    q = _rope(q_ref[...], *tables(cq_ref)) * scale
    kb = [_rope(kp_ref[...], *tables(cp_ref)), _rope(kc_ref[...], *tables(cq_ref)), _rope(kn_ref[...], *tables(cn_ref))]
    keys = jnp.concatenate([k.astype(BF16) for k in kb] + [kx_ref[...].astype(BF16)], axis=0)
    vals = jnp.concatenate([vp_ref[...].astype(BF16), vc_ref[...].astype(BF16), vn_ref[...].astype(BF16),
                            vx_ref[...].astype(BF16)], axis=0)
    nk = keys.shape[0]
    qi = lax.broadcasted_iota(jnp.int32, (rows, nk), 0)
    kj = lax.broadcasted_iota(jnp.int32, (rows, nk), 1)
    rel = kj - qi
    lo = jnp.where(n > 0, 0, rows)
    hi = jnp.where(n < n_blocks - 1, 3 * rows, 2 * rows)
    ok = (rel >= 0) & (rel <= 2 * WINDOW) & (kj >= lo) & (kj < hi)
    ok = ok | (kj >= 3 * rows)
    bias = jnp.where(ok, 0.0, NEG_INF).astype(F32)
    _attn_core(q, keys, vals, bias, sink_ref, o_ref)


def _attn_ctx_kernel(sink_ref, q_ref, kx_ref, vx_ref, o_ref):
    q = q_ref[...] * (HEAD_DIM ** -0.5)
    _attn_core(q, kx_ref[...].astype(BF16), vx_ref[...].astype(BF16), None, sink_ref, o_ref)


def _rope_tables(s):
    t = jnp.arange(s)
    row = (t // GRID_W).astype(F32)[:, None]
    col = (t % GRID_W).astype(F32)[:, None]
    half = HEAD_DIM // 2
    inv = ROPE_BASE ** (-jnp.arange(0, half, 2, dtype=F32) / half)
    lane = jnp.arange(2 * HEAD_DIM)
    d = lane % HEAD_DIM
    freq = inv[(d % half) % (half // 2)][None, :]
    ang = jnp.where((d < half)[None, :], row * freq, col * freq)
    first = ((d % half) < half // 2)[None, :]
    sin = jnp.sin(ang)
    return jnp.stack([jnp.cos(ang), jnp.where(first, -sin, 0.0), jnp.where(first, 0.0, sin)]).astype(F32)


def _attention(z, sink, rope_tab, lay):
    n_tok = z.shape[0]
    b, s, n_ctx, nc = lay["b"], lay["s"], lay["n_ctx"], lay["nc"]
    nb = s // CHUNK
    cb0 = nc // CHUNK
    kvw = N_KV_HEADS * HEAD_DIM
    qcol, kcol, vcol = COL_Q // D_BRANCH, COL_K // kvw, COL_V // kvw
    lat = lambda bb, n: cb0 + bb * nb + n
    prev = lambda n: jnp.maximum(n - 1, 0)
    nxt = lambda n: jnp.minimum(n + 1, nb - 1)
    smem = pl.BlockSpec(memory_space=pltpu.SMEM)
    out_lat = pl.pallas_call(
        functools.partial(_attn_latent_kernel, n_blocks=nb),
        out_shape=jax.ShapeDtypeStruct((b * s, D_BRANCH), BF16),
        grid=(b, nb),
        in_specs=[
            smem,
            pl.BlockSpec((CHUNK, D_BRANCH), lambda bb, n: (lat(bb, n), qcol)),
            pl.BlockSpec((CHUNK, kvw), lambda bb, n: (lat(bb, prev(n)), kcol)),
            pl.BlockSpec((CHUNK, kvw), lambda bb, n: (lat(bb, n), kcol)),
            pl.BlockSpec((CHUNK, kvw), lambda bb, n: (lat(bb, nxt(n)), kcol)),
            pl.BlockSpec((CHUNK, kvw), lambda bb, n: (lat(bb, prev(n)), vcol)),
            pl.BlockSpec((CHUNK, kvw), lambda bb, n: (lat(bb, n), vcol)),
            pl.BlockSpec((CHUNK, kvw), lambda bb, n: (lat(bb, nxt(n)), vcol)),
            pl.BlockSpec((n_ctx, kvw), lambda bb, n: (bb, kcol)),
            pl.BlockSpec((n_ctx, kvw), lambda bb, n: (bb, vcol)),
            pl.BlockSpec((3, CHUNK, kvw), lambda bb, n: (0, n, 0)),
            pl.BlockSpec((3, CHUNK, kvw), lambda bb, n: (0, prev(n), 0)),
            pl.BlockSpec((3, CHUNK, kvw), lambda bb, n: (0, nxt(n), 0)),
        ],
        out_specs=pl.BlockSpec((CHUNK, D_BRANCH), lambda bb, n: (bb * nb + n, 0)),
        compiler_params=_cparams("arbitrary", "arbitrary"),
        name="attention_latent",
    )(sink, z, z, z, z, z, z, z, z, z, rope_tab, rope_tab, rope_tab)
    ncb = n_ctx // CHUNK
    out_ctx = pl.pallas_call(
        _attn_ctx_kernel,
        out_shape=jax.ShapeDtypeStruct((nc, D_BRANCH), BF16),
        grid=(b, ncb),
        in_specs=[
            smem,
            pl.BlockSpec((CHUNK, D_BRANCH), lambda bb, n: (bb * ncb + n, qcol)),
            pl.BlockSpec((n_ctx, kvw), lambda bb, n: (bb, kcol)),
            pl.BlockSpec((n_ctx, kvw), lambda bb, n: (bb, vcol)),
        ],
        out_specs=pl.BlockSpec((CHUNK, D_BRANCH), lambda bb, n: (bb * ncb + n, 0)),
        compiler_params=_cparams("arbitrary", "arbitrary"),
        name="attention_ctx",
    )(sink, z, z, z)
    return jnp.concatenate([out_ctx, out_lat], axis=0)


def _s5_tables(a_re, a_im, log_step, b_re, b_im, c_re, c_im):
    t_len = CHUNK
    hi = lax.Precision.HIGHEST
    j = jnp.arange(t_len + 1, dtype=F32)
    b = lax.complex(b_re.astype(F32), b_im.astype(F32))
    pw, bbar, cc = [], [], []
    for d in range(2):
        lam = lax.complex(a_re[d].astype(F32), a_im[d].astype(F32))
        dt = jnp.exp(log_step[d].astype(F32))[:, None]
        log_lbar = lam * dt
        lbar = jnp.exp(log_lbar)
        pw.append(jnp.exp(log_lbar[:, None, :] * j[None, :, None]))
        bbar.append(((lbar - 1.0) / lam)[..., None] * b)
        cc.append(lax.complex(c_re[d].astype(F32), c_im[d].astype(F32)))

    def taps(d):
        cp = cc[d][:, None, :, :] * pw[d][:, :t_len, None, :]
        return (jnp.einsum("gjcp,gpk->gjck", cp.real, bbar[d].real, precision=hi)
                - jnp.einsum("gjcp,gpk->gjck", cp.imag, bbar[d].imag, precision=hi))

    kf, kb = taps(0), taps(1)
    kc = jnp.concatenate([kb[:, :0:-1], (kf[:, :1] + kb[:, :1]), kf[:, 1:]], axis=1)
    strip = kc.transpose(0, 3, 1, 2).reshape(S5_GROUPS, S5_GW, (2 * t_len - 1) * S5_GW)
    width = 2 * t_len * S5_GW
    strip = jnp.pad(strip, ((0, 0), (0, 0), (0, width + 128 - strip.shape[-1])))
    strips = jnp.stack([strip[:, :, r * S5_GW:r * S5_GW + width] for r in range(8)], axis=1)

    wf = pw[0][:, t_len - 1::-1][:, :t_len, :, None] * bbar[0][:, None, :, :]
    wb = pw[1][:, :t_len, :, None] * bbar[1][:, None, :, :]
    to_rows = lambda w: w.transpose(0, 1, 3, 2).reshape(S5_GROUPS, t_len * S5_GW, S5_STATE)
    wst = jnp.concatenate([to_rows(wf.real), to_rows(wb.real), to_rows(wf.imag), to_rows(wb.imag)], axis=-1)

    clf = cc[0][:, None, :, :] * pw[0][:, 1:t_len + 1, None, :]
    clb = cc[1][:, None, :, :] * pw[1][:, t_len:0:-1, None, :]
    to_cols = lambda m: m.reshape(S5_GROUPS, t_len * S5_GW, S5_STATE).transpose(0, 2, 1)
    cl = jnp.concatenate([to_cols(clf.real), to_cols(clb.real), to_cols(-clf.imag), to_cols(-clb.imag)], axis=1)

    dre = jnp.concatenate([pw[0][:, t_len].real, pw[1][:, t_len].real], axis=-1)
    dim = jnp.concatenate([pw[0][:, t_len].imag, pw[1][:, t_len].imag], axis=-1)
    decay = jnp.stack([dre, dim], axis=1)
    return strips.astype(BF16), wst.astype(BF16), cl.astype(BF16), decay.astype(F32)


def _s5_kernel(u_ref, strips_ref, wst_ref, cl_ref, dec_ref, y_ref, mt_ref, ds_ref, sin_ref, *, n_ctx_steps):
    t_len = CHUNK
    width = t_len * S5_GW
    n_steps = ds_ref.shape[0]
    for s in range(t_len):
        off = (t_len - 1 - s) * S5_GW
        r, q = (off // S5_GW) % 8, off // 128
        mt_ref[s * S5_GW:(s + 1) * S5_GW, :] = strips_ref[0, r, :, q * 128:q * 128 + width]
    u = u_ref[0]
    ds_ref[...] = jnp.dot(u, wst_ref[0], preferred_element_type=F32).reshape(ds_ref.shape)

    dre = jnp.broadcast_to(dec_ref[0, 0:1, :], (S5_PAD_B, 2 * S5_STATE))
    dim = jnp.broadcast_to(dec_ref[0, 1:2, :], (S5_PAD_B, 2 * S5_STATE))
    is_fwd = lax.broadcasted_iota(jnp.int32, (S5_PAD_B, 2 * S5_STATE), 1) < S5_STATE

    def step(it, carry):
        sre, sim = carry
        ib = jnp.where(it < n_ctx_steps, n_ctx_steps - 1 - it, n_steps - 1 + n_ctx_steps - it)
        sin_ref[it, :, 0:S5_STATE] = sre[:, 0:S5_STATE]
        sin_ref[it, :, 2 * S5_STATE:3 * S5_STATE] = sim[:, 0:S5_STATE]
        sin_ref[ib, :, S5_STATE:2 * S5_STATE] = sre[:, S5_STATE:]
        sin_ref[ib, :, 3 * S5_STATE:] = sim[:, S5_STATE:]
        df, db = ds_ref[it], ds_ref[ib]
        add_re = jnp.where(is_fwd, df[:, :2 * S5_STATE], db[:, :2 * S5_STATE])
        add_im = jnp.where(is_fwd, df[:, 2 * S5_STATE:], db[:, 2 * S5_STATE:])
        return sre * dre - sim * dim + add_re, sre * dim + sim * dre + add_im

    zero = jnp.zeros((S5_PAD_B, 2 * S5_STATE), F32)
    lax.fori_loop(0, n_steps, step, (zero, zero))

    sin = sin_ref[...].reshape(n_steps * S5_PAD_B, 4 * S5_STATE).astype(BF16)
    y = jnp.dot(u, mt_ref[...], preferred_element_type=F32)
    y_ref[0] = y + jnp.dot(sin, cl_ref[0], preferred_element_type=F32)


def _s5_scan(z, tables, lay):
    b, s, n_ctx, nc = lay["b"], lay["s"], lay["n_ctx"], lay["nc"]
    strips, wst, cl, decay = tables
    width = CHUNK * S5_GW
    ncs, nls = n_ctx // CHUNK, s // CHUNK
    n_steps = ncs + nls
    u = z[:, COL_D:COL_D + D_BRANCH].astype(BF16)

    def group_major(rows, steps):
        return rows.reshape(b, steps, CHUNK, S5_GROUPS, S5_GW).transpose(3, 1, 0, 2, 4).reshape(
            S5_GROUPS, steps, b, width)

    ug = jnp.concatenate([group_major(u[:nc], ncs), group_major(u[nc:], nls)], axis=1)
    ug = jnp.pad(ug, ((0, 0), (0, 0), (0, S5_PAD_B - b), (0, 0))).reshape(S5_GROUPS, n_steps * S5_PAD_B, width)
    m = n_steps * S5_PAD_B
    y = pl.pallas_call(
        functools.partial(_s5_kernel, n_ctx_steps=ncs),
        out_shape=jax.ShapeDtypeStruct((S5_GROUPS, m, width), F32),
        grid=(S5_GROUPS,),
        in_specs=[
            pl.BlockSpec((1, m, width), lambda g: (g, 0, 0)),
            pl.BlockSpec((1, 8, S5_GW, 2 * width), lambda g: (g, 0, 0, 0)),
            pl.BlockSpec((1, width, 4 * S5_STATE), lambda g: (g, 0, 0)),
            pl.BlockSpec((1, 4 * S5_STATE, width), lambda g: (g, 0, 0)),
            pl.BlockSpec((1, 2, 2 * S5_STATE), lambda g: (g, 0, 0)),
        ],
        out_specs=pl.BlockSpec((1, m, width), lambda g: (g, 0, 0)),
        scratch_shapes=[
            pltpu.VMEM((width, width), BF16),
            pltpu.VMEM((n_steps, S5_PAD_B, 4 * S5_STATE), F32),
            pltpu.VMEM((n_steps, S5_PAD_B, 4 * S5_STATE), F32),
        ],
        compiler_params=_cparams("arbitrary"),
        name="s5_scan",
    )(ug, strips, wst, cl, decay)
    y = y.reshape(S5_GROUPS, n_steps, S5_PAD_B, CHUNK, S5_GW)[:, :, :b]

    def token_major(part, steps):
        return part.transpose(2, 1, 3, 0, 4).reshape(b * steps * CHUNK, D_BRANCH)

    return jnp.concatenate([token_major(y[:, :ncs], ncs), token_major(y[:, ncs:], nls)], axis=0)


def _s5_readout_kernel(y_ref, u_ref, d_ref, w_ref, o_ref):
    y = _gelu_tanh(y_ref[...] + d_ref[...] * u_ref[...]).astype(BF16)
    r = jnp.dot(y, w_ref[...], preferred_element_type=F32)
    o_ref[...] = (r[:, :D_BRANCH] * _sigmoid(r[:, D_BRANCH:])).astype(o_ref.dtype)


def _s5_readout(y, z, d_skip, w_glu, lay):
    n_tok = y.shape[0]
    rows = lay["tm"]
    return pl.pallas_call(
        _s5_readout_kernel,
        out_shape=jax.ShapeDtypeStruct((n_tok, D_BRANCH), BF16),
        grid=(n_tok // rows,),
        in_specs=[
            pl.BlockSpec((rows, D_BRANCH), lambda i: (i, 0)),
            pl.BlockSpec((rows, D_BRANCH), lambda i: (i, COL_D // D_BRANCH)),
            pl.BlockSpec((1, D_BRANCH), lambda i: (0, 0)),
            pl.BlockSpec((D_BRANCH, 2 * D_BRANCH), lambda i: (0, 0)),
        ],
        out_specs=pl.BlockSpec((rows, D_BRANCH), lambda i: (i, 0)),
        compiler_params=_cparams("arbitrary"),
        name="s5_readout",
    )(y, z, d_skip.reshape(1, -1), w_glu)


def _merge_kernel(h_ref, ba_ref, bb_ref, bc_ref, bd_ref, wg_ref, bg_ref, wb_ref, o_ref):
    h = h_ref[...]
    acc = None
    for k, br_ref in enumerate((ba_ref, bb_ref, bc_ref, bd_ref)):
        gate = _sigmoid(jnp.dot(h, wg_ref[k], preferred_element_type=F32) + bg_ref[k])
        term = gate * jnp.dot(br_ref[...], wb_ref[k], preferred_element_type=F32)
        acc = term if acc is None else acc + term
    o_ref[...] = acc.astype(o_ref.dtype)


def _merge(h, branches, w_gate, b_gate, w_branch, lay):
    n_tok, d = h.shape
    tm, tn = lay["tm"], 512
    br_spec = pl.BlockSpec((tm, D_BRANCH), lambda i, j: (i, 0))
    return pl.pallas_call(
        _merge_kernel,
        out_shape=jax.ShapeDtypeStruct((n_tok, d), BF16),
        grid=(n_tok // tm, d // tn),
        in_specs=[
            pl.BlockSpec((tm, d), lambda i, j: (i, 0)),
            br_spec, br_spec, br_spec, br_spec,
            pl.BlockSpec((N_BRANCH, d, tn), lambda i, j: (0, 0, j)),
            pl.BlockSpec((N_BRANCH, 1, tn), lambda i, j: (0, 0, j)),
            pl.BlockSpec((N_BRANCH, D_BRANCH, tn), lambda i, j: (0, 0, j)),
        ],
        out_specs=pl.BlockSpec((tm, tn), lambda i, j: (i, j)),
        compiler_params=_cparams("arbitrary", "arbitrary"),
        name="branch_merge",
    )(h, *branches, w_gate, b_gate.reshape(N_BRANCH, 1, d), w_branch)


def _outproj_kernel(m_ref, w_ref, x_ref, gate_ref, o_ref):
    o_ref[...] = x_ref[...] + gate_ref[0] * jnp.dot(m_ref[...], w_ref[...], preferred_element_type=F32)


def _out_projection(merged, w_out, xall, mod3, lay):
    n_tok, d = xall.shape
    tm, tn = lay["tm"], 512
    mod_row = lay["mod_row"]
    return pl.pallas_call(
        _outproj_kernel,
        out_shape=jax.ShapeDtypeStruct((n_tok, d), F32),
        grid=(n_tok // tm, d // tn),
        in_specs=[
            pl.BlockSpec((tm, d), lambda i, j: (i, 0)),
            pl.BlockSpec((d, tn), lambda i, j: (0, j)),
            pl.BlockSpec((tm, tn), lambda i, j: (i, j)),
            pl.BlockSpec((1, 1, tn), lambda i, j: (mod_row(i, 2), 0, j)),
        ],
        out_specs=pl.BlockSpec((tm, tn), lambda i, j: (i, j)),
        compiler_params=_cparams("arbitrary", "arbitrary"),
        name="out_projection",
    )(merged, w_out, xall, mod3)


def _ffn_kernel(x_ref, g_ref, shift_ref, scale_ref, gate_ref, w1_ref, w2_ref, o_ref, h_ref):
    j = pl.program_id(1)

    @pl.when(j == 0)
    def _():
        x = x_ref[...]
        y = x * lax.rsqrt(jnp.mean(x * x, axis=-1, keepdims=True) + EPS) * g_ref[...]
        h_ref[...] = (y * (1.0 + scale_ref[0]) + shift_ref[0]).astype(BF16)
        o_ref[...] = jnp.zeros_like(o_ref)

    a = jnp.maximum(jnp.dot(h_ref[...], w1_ref[...], preferred_element_type=F32), 0.0)
    o_ref[...] += jnp.dot((a * a).astype(BF16), w2_ref[...], preferred_element_type=F32)

    @pl.when(j == pl.num_programs(1) - 1)
    def _():
        o_ref[...] = x_ref[...] + gate_ref[0] * o_ref[...]


def _ffn(xall, norm_g, mod3, w1, w2, lay):
    n_tok, d = xall.shape
    tm, tf = lay["tm"], 512
    mod_row = lay["mod_row"]
    mspec = lambda which: pl.BlockSpec((1, 1, d), lambda i, j: (mod_row(i, which), 0, 0))
    return pl.pallas_call(
        _ffn_kernel,
        out_shape=jax.ShapeDtypeStruct((n_tok, d), F32),
        grid=(n_tok // tm, D_FF // tf),
        in_specs=[
            pl.BlockSpec((tm, d), lambda i, j: (i, 0)),
            pl.BlockSpec((1, d), lambda i, j: (0, 0)),
            mspec(3), mspec(4), mspec(5),
            pl.BlockSpec((d, tf), lambda i, j: (0, j)),
            pl.BlockSpec((tf, d), lambda i, j: (j, 0)),
        ],
        out_specs=pl.BlockSpec((tm, d), lambda i, j: (i, 0)),
        scratch_shapes=[pltpu.VMEM((tm, d), BF16)],
        compiler_params=_cparams("arbitrary", "arbitrary"),
        name="ffn",
    )(xall, norm_g.reshape(1, d), mod3, mod3, mod3, w1, w2)


def _final_norm_kernel(x_ref, g_ref, o_ref):
    x = x_ref[...]
    o_ref[...] = x * lax.rsqrt(jnp.mean(x * x, axis=-1, keepdims=True) + EPS) * g_ref[...]


def _final_norm(xall, g, lay):
    d = xall.shape[1]
    tm = lay["tm"]
    first = lay["nc"] // tm
    n_lat = lay["b"] * lay["s"]
    return pl.pallas_call(
        _final_norm_kernel,
        out_shape=jax.ShapeDtypeStruct((n_lat, d), F32),
        grid=(n_lat // tm,),
        in_specs=[pl.BlockSpec((tm, d), lambda i: (i + first, 0)), pl.BlockSpec((1, d), lambda i: (0, 0))],
        out_specs=pl.BlockSpec((tm, d), lambda i: (i, 0)),
        compiler_params=_cparams("arbitrary"),
        name="final_norm",
    )(xall, g.reshape(1, d))


def _reorder_in_columns(w_in):
    q_end = 5 * D_BRANCH
    kv = 2 * N_KV_HEADS * HEAD_DIM
    return jnp.concatenate([w_in[..., :q_end], w_in[..., q_end + kv:], w_in[..., q_end:q_end + kv]], axis=-1)


def kernel(x, c, ctx, c_ctx, w_mod, b_mod, norm1_g, norm2_g, w_in, gmlp_ln_g, gmlp_ln_b, gmlp_ws, gmlp_bs, conv_w, conv_b, conv_ln_g, conv_ln_b, attn_sink, s5_a_re, s5_a_im, s5_log_step, s5_b_re, s5_b_im, s5_c_re, s5_c_im, s5_d, s5_w_glu, w_branch, w_gate, b_gate, w_out, w_ff1, w_ff2, final_g):
    b, s, d = x.shape
    n_ctx = ctx.shape[1]
    depth = w_mod.shape[0]
    assert d == D_MODEL and b <= S5_PAD_B and s % GRID_W == 0
    nc = b * n_ctx
    tm = _row_tile(nc, s)
    tseq = min(256, tm)
    nct, lpt = nc // tm, s // tm

    def mod_row(i, which):
        bidx = jnp.where(i < nct, b, (i - nct) // lpt)
        return bidx * N_MOD + which

    lay = dict(b=b, s=s, n_ctx=n_ctx, nc=nc, tm=tm, tseq=tseq, mod_row=mod_row)

    cond = jnp.concatenate([c, c_ctx[None, :]], axis=0)
    cond = jnp.pad(cond, ((0, -cond.shape[0] % 8), (0, 0)))
    mod_all = _modulation(cond, w_mod, b_mod)
    rope_tab = _rope_tables(s)
    xall = jnp.concatenate([ctx.reshape(nc, d), x.reshape(b * s, d)], axis=0)

    for l in range(depth):
        mod3 = mod_all[l, :b + 1].reshape((b + 1) * N_MOD, 1, d)
        w_in_l = _reorder_in_columns(w_in[l]).astype(BF16)
        h, z = _in_projection(xall, norm1_g[l], mod3, w_in_l, lay)
        br_a = _gmlp(z, gmlp_ln_g[l], gmlp_ln_b[l], gmlp_ws[l], gmlp_bs[l], lay)
        br_b = _conformer_conv(z, conv_w[l], conv_b[l], conv_ln_g[l], conv_ln_b[l], lay)
        br_c = _attention(z, attn_sink[l], rope_tab, lay)
        tables = _s5_tables(s5_a_re[l], s5_a_im[l], s5_log_step[l], s5_b_re[l], s5_b_im[l], s5_c_re[l], s5_c_im[l])
        y = _s5_scan(z, tables, lay)
        br_d = _s5_readout(y, z, s5_d[l], s5_w_glu[l].astype(BF16), lay)
        merged = _merge(h, (br_a, br_b, br_c, br_d), w_gate[l].astype(BF16), b_gate[l], w_branch[l].astype(BF16), lay)
        xall = _out_projection(merged, w_out[l].astype(BF16), xall, mod3, lay)
        xall = _ffn(xall, norm2_g[l], mod3, w_ff1[l].astype(BF16), w_ff2[l].astype(BF16), lay)

    return _final_norm(xall, final_g, lay).reshape(b, s, d)
```

```python
import functools
import math

import jax
import jax.numpy as jnp
from jax import lax
from jax.experimental import pallas as pl
from jax.experimental.pallas import tpu as pltpu

D_MODEL = 2048
N_BRANCH = 4
D_BRANCH = D_MODEL // N_BRANCH
CHUNK = 128
GMLP_GROUPS = 4
CONV_W = 31
CONV_HALO = 16
HEAD_DIM = 64
N_Q_HEADS = D_BRANCH // HEAD_DIM
N_KV_HEADS = 2
Q_PER_KV = N_Q_HEADS // N_KV_HEADS
WINDOW = 128
ROPE_BASE = 10000.0
GRID_W = 64
S5_GW = 16
S5_GROUPS = D_BRANCH // S5_GW
S5_STATE = 64
D_FF = 4 * D_MODEL
N_MOD = 6
EPS = 1e-6
NEG_INF = -1e30

COL_A = 0
COL_B = 2 * D_BRANCH
COL_Q = 4 * D_BRANCH
COL_D = 5 * D_BRANCH
COL_K = 6 * D_BRANCH
COL_V = COL_K + N_KV_HEADS * HEAD_DIM
IN_COLS = COL_V + N_KV_HEADS * HEAD_DIM
IN_COLS_PAD = -(-IN_COLS // D_BRANCH) * D_BRANCH

VMEM_LIMIT_V7X = 56 * 1024 * 1024

F32 = jnp.float32
BF16 = jnp.bfloat16


def _cparams(*sem):
    return pltpu.CompilerParams(dimension_semantics=sem, vmem_limit_bytes=VMEM_LIMIT_V7X)


def _gelu_tanh(x):
    return 0.5 * x * (1.0 + jnp.tanh(math.sqrt(2.0 / math.pi) * (x + 0.044715 * (x * x * x))))


def _sigmoid(x):
    return 1.0 / (1.0 + jnp.exp(-x))


def _layer_norm_rows(v, g, b):
    mu = jnp.mean(v, axis=-1, keepdims=True)
    vc = v - mu
    var = jnp.mean(vc * vc, axis=-1, keepdims=True)
    return vc * lax.rsqrt(var + EPS) * g + b


def _row_tile(nc, s):
    for t in (1024, 512, 256, 128):
        if nc % t == 0 and s % t == 0:
            return t
    raise ValueError("context and sequence lengths must be multiples of 128")


def _mod_kernel(c_ref, w_ref, b_ref, o_ref):
    c = c_ref[...]
    c = (c * _sigmoid(c)).astype(BF16)
    o_ref[0] = jnp.dot(c, w_ref[0].astype(BF16), preferred_element_type=F32) + b_ref[0]


def _modulation(cond, w_mod, b_mod):
    depth, d, n = w_mod.shape
    rows = cond.shape[0]
    tn = 1024
    return pl.pallas_call(
        _mod_kernel,
        out_shape=jax.ShapeDtypeStruct((depth, rows, n), F32),
        grid=(depth, n // tn),
        in_specs=[
            pl.BlockSpec((rows, d), lambda l, j: (0, 0)),
            pl.BlockSpec((1, d, tn), lambda l, j: (l, 0, j)),
            pl.BlockSpec((1, 1, tn), lambda l, j: (l, 0, j)),
        ],
        out_specs=pl.BlockSpec((1, rows, tn), lambda l, j: (l, 0, j)),
        compiler_params=_cparams("arbitrary", "arbitrary"),
        name="modulation",
    )(cond, w_mod, b_mod.reshape(depth, 1, n))


NORM_ROWS = 128


def _modulated_rms_norm(x_ref, g_ref, shift_ref, scale_ref, h_ref):
    gain = g_ref[...] * (1.0 + scale_ref[0])
    shift = shift_ref[0]

    def body(r, carry):
        rows = pl.ds(pl.multiple_of(r * NORM_ROWS, NORM_ROWS), NORM_ROWS)
        x = x_ref[rows, :]
        inv = lax.rsqrt(jnp.mean(x * x, axis=-1, keepdims=True) + EPS)
        h_ref[rows, :] = (x * inv * gain + shift).astype(h_ref.dtype)
        return carry

    lax.fori_loop(0, x_ref.shape[0] // NORM_ROWS, body, 0)


def _inproj_kernel(x_ref, g_ref, shift_ref, scale_ref, w_ref, h_ref, z_ref, ud_ref, *, d_tile):
    j = pl.program_id(1)

    @pl.when(j == 0)
    def _():
        _modulated_rms_norm(x_ref, g_ref, shift_ref, scale_ref, h_ref)

    z = jnp.dot(h_ref[...], w_ref[...], preferred_element_type=F32)
    z_ref[...] = z

    @pl.when(j == d_tile)
    def _():
        ud_ref[...] = z.astype(ud_ref.dtype)


def _in_projection(xall, norm_g, mod3, w_in, lay):
    n_tok, d = xall.shape
    tm, tn = lay["tm"], D_BRANCH
    mod_row = lay["mod_row"]
    return pl.pallas_call(
        functools.partial(_inproj_kernel, d_tile=COL_D // tn),
        out_shape=(jax.ShapeDtypeStruct((n_tok, d), BF16), jax.ShapeDtypeStruct((n_tok, IN_COLS_PAD), F32),
                   jax.ShapeDtypeStruct((n_tok, D_BRANCH), BF16)),
        grid=(n_tok // tm, IN_COLS_PAD // tn),
        in_specs=[
            pl.BlockSpec((tm, d), lambda i, j: (i, 0)),
            pl.BlockSpec((1, d), lambda i, j: (0, 0)),
            pl.BlockSpec((1, 1, d), lambda i, j: (mod_row(i, 0), 0, 0)),
            pl.BlockSpec((1, 1, d), lambda i, j: (mod_row(i, 1), 0, 0)),
            pl.BlockSpec((d, tn), lambda i, j: (0, j)),
        ],
        out_specs=(pl.BlockSpec((tm, d), lambda i, j: (i, 0)), pl.BlockSpec((tm, tn), lambda i, j: (i, j)),
                   pl.BlockSpec((tm, D_BRANCH), lambda i, j: (i, 0))),
        compiler_params=_cparams("arbitrary", "arbitrary"),
        name="in_projection",
    )(xall, norm_g.reshape(1, d), mod3, mod3, w_in)


def _gmlp_kernel(za_ref, lng_ref, lnb_ref, ws_ref, bs_ref, o_ref):
    rows = za_ref.shape[0]
    gw = D_BRANCH // GMLP_GROUPS
    a = _gelu_tanh(za_ref[...])
    u = a[:, :D_BRANCH]
    v = _layer_norm_rows(a[:, D_BRANCH:], lng_ref[...], lnb_ref[...]).astype(BF16)
    for c in range(rows // CHUNK):
        r0 = c * CHUNK
        for g in range(GMLP_GROUPS):
            c0 = g * gw
            mixed = jnp.dot(ws_ref[g], v[r0:r0 + CHUNK, c0:c0 + gw], preferred_element_type=F32) + bs_ref[g]
            o_ref[r0:r0 + CHUNK, c0:c0 + gw] = (u[r0:r0 + CHUNK, c0:c0 + gw] * mixed).astype(o_ref.dtype)


def _gmlp(z, ln_g, ln_b, ws, bs, lay):
    n_tok = z.shape[0]
    rows = lay["tseq"]
    gw = D_BRANCH // GMLP_GROUPS
    bs_full = jnp.broadcast_to(bs[:, :, None], (GMLP_GROUPS, CHUNK, gw)).astype(F32)
    return pl.pallas_call(
        _gmlp_kernel,
        out_shape=jax.ShapeDtypeStruct((n_tok, D_BRANCH), BF16),
        grid=(n_tok // rows,),
        in_specs=[
            pl.BlockSpec((rows, 2 * D_BRANCH), lambda i: (i, COL_A // (2 * D_BRANCH))),
            pl.BlockSpec((1, D_BRANCH), lambda i: (0, 0)),
            pl.BlockSpec((1, D_BRANCH), lambda i: (0, 0)),
            pl.BlockSpec((GMLP_GROUPS, CHUNK, CHUNK), lambda i: (0, 0, 0)),
            pl.BlockSpec((GMLP_GROUPS, CHUNK, gw), lambda i: (0, 0, 0)),
        ],
        out_specs=pl.BlockSpec((rows, D_BRANCH), lambda i: (i, 0)),
        compiler_params=_cparams("arbitrary"),
        name="gmlp_mix",
    )(z, ln_g.reshape(1, -1), ln_b.reshape(1, -1), ws.astype(BF16), bs_full)


def _conv_kernel(zm_ref, zp_ref, zn_ref, w_ref, cb_ref, lng_ref, lnb_ref, o_ref, ybuf, *, seg_tiles):
    rows = zm_ref.shape[0]
    i = pl.program_id(0)
    nct, cpt, lpt = seg_tiles
    pos = jnp.where(i < nct, i % cpt, (i - nct) % lpt)
    seg = jnp.where(i < nct, cpt, lpt)
    keep_prev = (pos != 0).astype(F32)
    keep_next = (pos != seg - 1).astype(F32)

    def glu(ref):
        z = ref[...]
        return z[:, :D_BRANCH] * _sigmoid(z[:, D_BRANCH:])

    ybuf[0:CONV_HALO, :] = glu(zp_ref) * keep_prev
    ybuf[CONV_HALO:CONV_HALO + rows, :] = glu(zm_ref)
    ybuf[CONV_HALO + rows:, :] = glu(zn_ref) * keep_next

    sub = 32
    off = CONV_HALO - CONV_W // 2
    for r in range(rows // sub):
        acc = jnp.zeros((sub, D_BRANCH), F32)
        for k in range(CONV_W):
            acc = acc + ybuf[r * sub + off + k:r * sub + off + k + sub, :] * w_ref[k:k + 1, :]
        y = _layer_norm_rows(acc + cb_ref[...], lng_ref[...], lnb_ref[...])
        o_ref[r * sub:(r + 1) * sub, :] = (y * _sigmoid(y)).astype(o_ref.dtype)


def _conformer_conv(z, w_dw, b_dw, ln_g, ln_b, lay):
    n_tok = z.shape[0]
    rows = lay["tseq"]
    hb = rows // CONV_HALO
    n_halo_blocks = n_tok // CONV_HALO
    colb = COL_B // (2 * D_BRANCH)
    seg_tiles = (lay["nc"] // rows, lay["n_ctx"] // rows, lay["s"] // rows)
    vec = lambda a: a.reshape(1, -1)
    return pl.pallas_call(
        functools.partial(_conv_kernel, seg_tiles=seg_tiles),
        out_shape=jax.ShapeDtypeStruct((n_tok, D_BRANCH), BF16),
        grid=(n_tok // rows,),
        in_specs=[
            pl.BlockSpec((rows, 2 * D_BRANCH), lambda i: (i, colb)),
            pl.BlockSpec((CONV_HALO, 2 * D_BRANCH), lambda i: (jnp.maximum(i * hb - 1, 0), colb)),
            pl.BlockSpec((CONV_HALO, 2 * D_BRANCH), lambda i: (jnp.minimum((i + 1) * hb, n_halo_blocks - 1), colb)),
            pl.BlockSpec((CONV_W, D_BRANCH), lambda i: (0, 0)),
            pl.BlockSpec((1, D_BRANCH), lambda i: (0, 0)),
            pl.BlockSpec((1, D_BRANCH), lambda i: (0, 0)),
            pl.BlockSpec((1, D_BRANCH), lambda i: (0, 0)),
        ],
        out_specs=pl.BlockSpec((rows, D_BRANCH), lambda i: (i, 0)),
        scratch_shapes=[pltpu.VMEM((rows + 2 * CONV_HALO, D_BRANCH), F32)],
        compiler_params=_cparams("arbitrary"),
        name="conformer_conv",
    )(z, z, z, w_dw, vec(b_dw), vec(ln_g), vec(ln_b))


def _rope(x, cos, sin_a, sin_b):
    reps = x.shape[1] // cos.shape[1]
    if reps > 1:
        cos, sin_a, sin_b = (jnp.concatenate([t] * reps, axis=1) for t in (cos, sin_a, sin_b))
    n = x.shape[1]
    half = HEAD_DIM // 4
    return x * cos + pltpu.roll(x, n - half, 1) * sin_a + pltpu.roll(x, half, 1) * sin_b


def _attn_core(q, keys, vals, bias, sink_ref, o_ref):
    rows = q.shape[0]
    rid = lax.broadcasted_iota(jnp.int32, (Q_PER_KV * rows, 1), 0)
    for h in range(N_KV_HEADS):
        kh = keys[:, h * HEAD_DIM:(h + 1) * HEAD_DIM]
        vh = vals[:, h * HEAD_DIM:(h + 1) * HEAD_DIM]
        heads = [q[:, (h * Q_PER_KV + g) * HEAD_DIM:(h * Q_PER_KV + g + 1) * HEAD_DIM] for g in range(Q_PER_KV)]
        qh = jnp.concatenate(heads, axis=0).astype(BF16)
        s = lax.dot_general(qh, kh, (((1,), (1,)), ((), ())), preferred_element_type=F32)
        if bias is not None:
            s = s + jnp.concatenate([bias] * Q_PER_KV, axis=0)
        sink = jnp.full((Q_PER_KV * rows, 1), sink_ref[h * Q_PER_KV], F32)
        for g in range(1, Q_PER_KV):
            sink = jnp.where(rid >= g * rows, sink_ref[h * Q_PER_KV + g], sink)
        m = jnp.maximum(jnp.max(s, axis=-1, keepdims=True), sink)
        e = jnp.exp(s - m)
        denom = jnp.sum(e, axis=-1, keepdims=True) + jnp.exp(sink - m)
        o = jnp.dot(e.astype(BF16), vh, preferred_element_type=F32) * (1.0 / denom)
        for g in range(Q_PER_KV):
            c0 = (h * Q_PER_KV + g) * HEAD_DIM
            o_ref[:, c0:c0 + HEAD_DIM] = o[g * rows:(g + 1) * rows, :].astype(o_ref.dtype)


def _attn_latent_kernel(sink_ref, q_ref, kp_ref, kc_ref, kn_ref, vp_ref, vc_ref, vn_ref, kx_ref, vx_ref,
                        cq_ref, cp_ref, cn_ref, o_ref, *, n_blocks):
    n = pl.program_id(1)
    rows = q_ref.shape[0]
    scale = HEAD_DIM ** -0.5

    def tables(ref):
        return ref[0], ref[1], ref[2]

    q = _rope(q_ref[...], *tables(cq_ref)) * scale
    kb = [_rope(kp_ref[...], *tables(cp_ref)), _rope(kc_ref[...], *tables(cq_ref)), _rope(kn_ref[...], *tables(cn_ref))]
    keys = jnp.concatenate([k.astype(BF16) for k in kb] + [kx_ref[...].astype(BF16)], axis=0)
    vals = jnp.concatenate([vp_ref[...].astype(BF16), vc_ref[...].astype(BF16), vn_ref[...].astype(BF16),
                            vx_ref[...].astype(BF16)], axis=0)
    nk = keys.shape[0]
    qi = lax.broadcasted_iota(jnp.int32, (rows, nk), 0)
    kj = lax.broadcasted_iota(jnp.int32, (rows, nk), 1)
    rel = kj - qi
    lo = jnp.where(n > 0, 0, rows)
    hi = jnp.where(n < n_blocks - 1, 3 * rows, 2 * rows)
    ok = (rel >= 0) & (rel <= 2 * WINDOW) & (kj >= lo) & (kj < hi)
    ok = ok | (kj >= 3 * rows)
    bias = jnp.where(ok, 0.0, NEG_INF).astype(F32)
    _attn_core(q, keys, vals, bias, sink_ref, o_ref)


def _attn_ctx_kernel(sink_ref, q_ref, kx_ref, vx_ref, latent_out_ref, o_ref):
    del latent_out_ref
    q = q_ref[...] * (HEAD_DIM ** -0.5)
    _attn_core(q, kx_ref[...].astype(BF16), vx_ref[...].astype(BF16), None, sink_ref, o_ref)


def _rope_tables(s):
    t = jnp.arange(s)
    row = (t // GRID_W).astype(F32)[:, None]
    col = (t % GRID_W).astype(F32)[:, None]
    half = HEAD_DIM // 2
    inv = ROPE_BASE ** (-jnp.arange(0, half, 2, dtype=F32) / half)
    lane = jnp.arange(2 * HEAD_DIM)
    d = lane % HEAD_DIM
    freq = inv[(d % half) % (half // 2)][None, :]
    ang = jnp.where((d < half)[None, :], row * freq, col * freq)
    first = ((d % half) < half // 2)[None, :]
    sin = jnp.sin(ang)
    return jnp.stack([jnp.cos(ang), jnp.where(first, -sin, 0.0), jnp.where(first, 0.0, sin)]).astype(F32)


def _attention(z, sink, rope_tab, lay):
    n_tok = z.shape[0]
    b, s, n_ctx, nc = lay["b"], lay["s"], lay["n_ctx"], lay["nc"]
    nb = s // CHUNK
    cb0 = nc // CHUNK
    kvw = N_KV_HEADS * HEAD_DIM
    qcol, kcol, vcol = COL_Q // D_BRANCH, COL_K // kvw, COL_V // kvw
    lat = lambda bb, n: cb0 + bb * nb + n
    prev = lambda n: jnp.maximum(n - 1, 0)
    nxt = lambda n: jnp.minimum(n + 1, nb - 1)
    smem = pl.BlockSpec(memory_space=pltpu.SMEM)
    out_lat = pl.pallas_call(
        functools.partial(_attn_latent_kernel, n_blocks=nb),
        out_shape=jax.ShapeDtypeStruct((n_tok, D_BRANCH), BF16),
        grid=(b, nb),
        in_specs=[
            smem,
            pl.BlockSpec((CHUNK, D_BRANCH), lambda bb, n: (lat(bb, n), qcol)),
            pl.BlockSpec((CHUNK, kvw), lambda bb, n: (lat(bb, prev(n)), kcol)),
            pl.BlockSpec((CHUNK, kvw), lambda bb, n: (lat(bb, n), kcol)),
            pl.BlockSpec((CHUNK, kvw), lambda bb, n: (lat(bb, nxt(n)), kcol)),
            pl.BlockSpec((CHUNK, kvw), lambda bb, n: (lat(bb, prev(n)), vcol)),
            pl.BlockSpec((CHUNK, kvw), lambda bb, n: (lat(bb, n), vcol)),
            pl.BlockSpec((CHUNK, kvw), lambda bb, n: (lat(bb, nxt(n)), vcol)),
            pl.BlockSpec((n_ctx, kvw), lambda bb, n: (bb, kcol)),
            pl.BlockSpec((n_ctx, kvw), lambda bb, n: (bb, vcol)),
            pl.BlockSpec((3, CHUNK, kvw), lambda bb, n: (0, n, 0)),
            pl.BlockSpec((3, CHUNK, kvw), lambda bb, n: (0, prev(n), 0)),
            pl.BlockSpec((3, CHUNK, kvw), lambda bb, n: (0, nxt(n), 0)),
        ],
        out_specs=pl.BlockSpec((CHUNK, D_BRANCH), lambda bb, n: (lat(bb, n), 0)),
        compiler_params=_cparams("arbitrary", "arbitrary"),
        name="attention_latent",
    )(sink, z, z, z, z, z, z, z, z, z, rope_tab, rope_tab, rope_tab)
    ncb = n_ctx // CHUNK
    return pl.pallas_call(
        _attn_ctx_kernel,
        out_shape=jax.ShapeDtypeStruct((n_tok, D_BRANCH), BF16),
        grid=(b, ncb),
        in_specs=[
            smem,
            pl.BlockSpec((CHUNK, D_BRANCH), lambda bb, n: (bb * ncb + n, qcol)),
            pl.BlockSpec((n_ctx, kvw), lambda bb, n: (bb, kcol)),
            pl.BlockSpec((n_ctx, kvw), lambda bb, n: (bb, vcol)),
            pl.BlockSpec(memory_space=pl.ANY),
        ],
        out_specs=pl.BlockSpec((CHUNK, D_BRANCH), lambda bb, n: (bb * ncb + n, 0)),
        input_output_aliases={4: 0},
        compiler_params=_cparams("arbitrary", "arbitrary"),
        name="attention_ctx",
    )(sink, z, z, z, out_lat)


def _s5_tables(a_re, a_im, log_step, b_re, b_im, c_re, c_im):
    t_len = CHUNK
    hi = lax.Precision.HIGHEST
    j = jnp.arange(t_len + 1, dtype=F32)
    b = lax.complex(b_re.astype(F32), b_im.astype(F32))
    pw, bbar, cc = [], [], []
    for d in range(2):
        lam = lax.complex(a_re[d].astype(F32), a_im[d].astype(F32))
        dt = jnp.exp(log_step[d].astype(F32))[:, None]
        log_lbar = lam * dt
        lbar = jnp.exp(log_lbar)
        pw.append(jnp.exp(log_lbar[:, None, :] * j[None, :, None]))
        bbar.append(((lbar - 1.0) / lam)[..., None] * b)
        cc.append(lax.complex(c_re[d].astype(F32), c_im[d].astype(F32)))

    def taps(d):
        cp = cc[d][:, None, :, :] * pw[d][:, :t_len, None, :]
        return (jnp.einsum("gjcp,gpk->gjck", cp.real, bbar[d].real, precision=hi)
                - jnp.einsum("gjcp,gpk->gjck", cp.imag, bbar[d].imag, precision=hi))

    kf, kb = taps(0), taps(1)
    kc = jnp.concatenate([kb[:, :0:-1], (kf[:, :1] + kb[:, :1]), kf[:, 1:]], axis=1)
    strip = kc.transpose(0, 3, 1, 2).reshape(S5_GROUPS, S5_GW, (2 * t_len - 1) * S5_GW)
    width = 2 * t_len * S5_GW
    strip = jnp.pad(strip, ((0, 0), (0, 0), (0, width + 128 - strip.shape[-1])))
    strips = jnp.stack([strip[:, :, r * S5_GW:r * S5_GW + width] for r in range(8)], axis=1)

    wf = pw[0][:, t_len - 1::-1][:, :t_len, :, None] * bbar[0][:, None, :, :]
    wb = pw[1][:, :t_len, :, None] * bbar[1][:, None, :, :]
    to_rows = lambda w: w.transpose(0, 1, 3, 2).reshape(S5_GROUPS, t_len * S5_GW, S5_STATE)
    wst = jnp.concatenate([to_rows(wf.real), to_rows(wb.real), to_rows(wf.imag), to_rows(wb.imag)], axis=-1)

    clf = cc[0][:, None, :, :] * pw[0][:, 1:t_len + 1, None, :]
    clb = cc[1][:, None, :, :] * pw[1][:, t_len:0:-1, None, :]
    to_cols = lambda m: m.reshape(S5_GROUPS, t_len * S5_GW, S5_STATE).transpose(0, 2, 1)
    cl = jnp.concatenate([to_cols(clf.real), to_cols(clb.real), to_cols(-clf.imag), to_cols(-clb.imag)], axis=1)

    dre = jnp.concatenate([pw[0][:, t_len].real, pw[1][:, t_len].real], axis=-1)
    dim = jnp.concatenate([pw[0][:, t_len].imag, pw[1][:, t_len].imag], axis=-1)
    decay = jnp.stack([dre, dim], axis=1)
    return strips.astype(BF16), wst.astype(BF16), cl.astype(BF16), decay.astype(F32)


def _s5_kernel(u_ref, strips_ref, wst_ref, cl_ref, dec_ref, dskip_ref, y_ref,
               mt_ref, dsre_ref, dsim_ref, fre_ref, fim_ref, bre_ref, bim_ref, *, batch, n_ctx_steps, n_lat_steps):
    t_len = CHUNK
    width = t_len * S5_GW
    lanes = 2 * S5_STATE
    for s in range(t_len):
        off = (t_len - 1 - s) * S5_GW
        r, q = (off // S5_GW) % 8, off // 128
        mt_ref[s * S5_GW:(s + 1) * S5_GW, :] = strips_ref[0, r, :, q * 128:q * 128 + width]
    u = u_ref[0]
    ds = jnp.dot(u, wst_ref[0], preferred_element_type=F32)
    dsre_ref[...] = ds[:, :lanes]
    dsim_ref[...] = ds[:, lanes:]

    dre = jnp.broadcast_to(dec_ref[0, 0:1, :], (batch, lanes))
    dim = jnp.broadcast_to(dec_ref[0, 1:2, :], (batch, lanes))
    is_fwd = lax.broadcasted_iota(jnp.int32, (batch, lanes), 1) < S5_STATE

    def run(first_row, n_steps, carry):
        def step(it, carry):
            sre, sim = carry
            rows_f = pl.ds(first_row + it, batch, stride=n_steps)
            rows_b = pl.ds(first_row + n_steps - 1 - it, batch, stride=n_steps)
            fre_ref[rows_f, :] = sre
            fim_ref[rows_f, :] = sim
            bre_ref[rows_b, :] = sre
            bim_ref[rows_b, :] = sim
            add_re = jnp.where(is_fwd, dsre_ref[rows_f, :], dsre_ref[rows_b, :])
            add_im = jnp.where(is_fwd, dsim_ref[rows_f, :], dsim_ref[rows_b, :])
            return sre * dre - sim * dim + add_re, sre * dim + sim * dre + add_im

        return lax.fori_loop(0, n_steps, step, carry)

    zero = jnp.zeros((batch, lanes), F32)
    carry = run(0, n_ctx_steps, (zero, zero))
    run(batch * n_ctx_steps, n_lat_steps, carry)

    n_rows = dsre_ref.shape[0]
    fwd_rows = lax.broadcasted_iota(jnp.int32, (n_rows, lanes), 1) < S5_STATE
    sin = jnp.concatenate([jnp.where(fwd_rows, fre_ref[...], bre_ref[...]),
                           jnp.where(fwd_rows, fim_ref[...], bim_ref[...])], axis=1).astype(BF16)
    y = jnp.dot(u, mt_ref[...], preferred_element_type=F32)
    y = y + jnp.dot(sin, cl_ref[0], preferred_element_type=F32)
    y_ref[0] = _gelu_tanh(y + dskip_ref[0] * u.astype(F32)).astype(y_ref.dtype)


def _s5_mix(ud, tables, d_skip, lay):
    b, s, n_ctx = lay["b"], lay["s"], lay["n_ctx"]
    strips, wst, cl, decay = tables
    width = CHUNK * S5_GW
    ncs, nls = n_ctx // CHUNK, s // CHUNK
    m = b * (ncs + nls)
    ug = ud.reshape(m, CHUNK, S5_GROUPS, S5_GW).transpose(2, 0, 1, 3).reshape(S5_GROUPS, m, width)
    dskip = jnp.broadcast_to(d_skip.astype(F32).reshape(S5_GROUPS, 1, 1, S5_GW),
                             (S5_GROUPS, 1, CHUNK, S5_GW)).reshape(S5_GROUPS, 1, width)
    state_rows = pltpu.VMEM((m, 2 * S5_STATE), F32)
    y = pl.pallas_call(
        functools.partial(_s5_kernel, batch=b, n_ctx_steps=ncs, n_lat_steps=nls),
        out_shape=jax.ShapeDtypeStruct((S5_GROUPS, m, width), BF16),
        grid=(S5_GROUPS,),
        in_specs=[
            pl.BlockSpec((1, m, width), lambda g: (g, 0, 0)),
            pl.BlockSpec((1, 8, S5_GW, 2 * width), lambda g: (g, 0, 0, 0)),
            pl.BlockSpec((1, width, 4 * S5_STATE), lambda g: (g, 0, 0)),
            pl.BlockSpec((1, 4 * S5_STATE, width), lambda g: (g, 0, 0)),
            pl.BlockSpec((1, 2, 2 * S5_STATE), lambda g: (g, 0, 0)),
            pl.BlockSpec((1, 1, width), lambda g: (g, 0, 0)),
        ],
        out_specs=pl.BlockSpec((1, m, width), lambda g: (g, 0, 0)),
        scratch_shapes=[
            pltpu.VMEM((width, width), BF16),
            state_rows, state_rows, state_rows, state_rows, state_rows, state_rows,
        ],
        compiler_params=_cparams("arbitrary"),
        name="s5_scan",
    )(ug, strips, wst, cl, decay, dskip)
    return y.reshape(S5_GROUPS, m, CHUNK, S5_GW).transpose(1, 2, 0, 3).reshape(m * CHUNK, D_BRANCH)


def _s5_readout_kernel(y_ref, w_ref, o_ref):
    r = jnp.dot(y_ref[...], w_ref[...], preferred_element_type=F32)
    o_ref[...] = (r[:, :D_BRANCH] * _sigmoid(r[:, D_BRANCH:])).astype(o_ref.dtype)


def _s5_readout(y, w_glu, lay):
    n_tok = y.shape[0]
    rows = lay["tm"]
    return pl.pallas_call(
        _s5_readout_kernel,
        out_shape=jax.ShapeDtypeStruct((n_tok, D_BRANCH), BF16),
        grid=(n_tok // rows,),
        in_specs=[
            pl.BlockSpec((rows, D_BRANCH), lambda i: (i, 0)),
            pl.BlockSpec((D_BRANCH, 2 * D_BRANCH), lambda i: (0, 0)),
        ],
        out_specs=pl.BlockSpec((rows, D_BRANCH), lambda i: (i, 0)),
        compiler_params=_cparams("arbitrary"),
        name="s5_readout",
    )(y, w_glu)


def _merge_kernel(h_ref, ba_ref, bb_ref, bc_ref, bd_ref, wg_ref, bg_ref, wb_ref, o_ref):
    h = h_ref[...]
    acc = None
    for k, br_ref in enumerate((ba_ref, bb_ref, bc_ref, bd_ref)):
        gate = _sigmoid(jnp.dot(h, wg_ref[k], preferred_element_type=F32) + bg_ref[k])
        term = gate * jnp.dot(br_ref[...], wb_ref[k], preferred_element_type=F32)
        acc = term if acc is None else acc + term
    o_ref[...] = acc.astype(o_ref.dtype)


def _merge(h, branches, w_gate, b_gate, w_branch, lay):
    n_tok, d = h.shape
    tm, tn = lay["tm"], 512
    br_spec = pl.BlockSpec((tm, D_BRANCH), lambda i, j: (i, 0))
    return pl.pallas_call(
        _merge_kernel,
        out_shape=jax.ShapeDtypeStruct((n_tok, d), BF16),
        grid=(n_tok // tm, d // tn),
        in_specs=[
            pl.BlockSpec((tm, d), lambda i, j: (i, 0)),
            br_spec, br_spec, br_spec, br_spec,
            pl.BlockSpec((N_BRANCH, d, tn), lambda i, j: (0, 0, j)),
            pl.BlockSpec((N_BRANCH, 1, tn), lambda i, j: (0, 0, j)),
            pl.BlockSpec((N_BRANCH, D_BRANCH, tn), lambda i, j: (0, 0, j)),
        ],
        out_specs=pl.BlockSpec((tm, tn), lambda i, j: (i, j)),
        compiler_params=_cparams("arbitrary", "arbitrary"),
        name="branch_merge",
    )(h, *branches, w_gate, b_gate.reshape(N_BRANCH, 1, d), w_branch)


def _outproj_kernel(m_ref, w_ref, x_ref, gate_ref, o_ref):
    o_ref[...] = x_ref[...] + gate_ref[0] * jnp.dot(m_ref[...], w_ref[...], preferred_element_type=F32)


def _out_projection(merged, w_out, xall, mod3, lay):
    n_tok, d = xall.shape
    tm, tn = lay["tm"], 512
    mod_row = lay["mod_row"]
    return pl.pallas_call(
        _outproj_kernel,
        out_shape=jax.ShapeDtypeStruct((n_tok, d), F32),
        grid=(n_tok // tm, d // tn),
        in_specs=[
            pl.BlockSpec((tm, d), lambda i, j: (i, 0)),
            pl.BlockSpec((d, tn), lambda i, j: (0, j)),
            pl.BlockSpec((tm, tn), lambda i, j: (i, j)),
            pl.BlockSpec((1, 1, tn), lambda i, j: (mod_row(i, 2), 0, j)),
        ],
        out_specs=pl.BlockSpec((tm, tn), lambda i, j: (i, j)),
        compiler_params=_cparams("arbitrary", "arbitrary"),
        name="out_projection",
    )(merged, w_out, xall, mod3)


def _ffn_kernel(x_ref, g_ref, shift_ref, scale_ref, gate_ref, w1_ref, w2_ref, o_ref, h_ref):
    j = pl.program_id(1)

    @pl.when(j == 0)
    def _():
        _modulated_rms_norm(x_ref, g_ref, shift_ref, scale_ref, h_ref)
        o_ref[...] = jnp.zeros_like(o_ref)

    a = jnp.maximum(jnp.dot(h_ref[...], w1_ref[...], preferred_element_type=F32), 0.0)
    o_ref[...] += jnp.dot((a * a).astype(BF16), w2_ref[...], preferred_element_type=F32)

    @pl.when(j == pl.num_programs(1) - 1)
    def _():
        o_ref[...] = x_ref[...] + gate_ref[0] * o_ref[...]


def _ffn(xall, norm_g, mod3, w1, w2, lay):
    n_tok, d = xall.shape
    tm, tf = lay["tm"], 512
    mod_row = lay["mod_row"]
    mspec = lambda which: pl.BlockSpec((1, 1, d), lambda i, j: (mod_row(i, which), 0, 0))
    return pl.pallas_call(
        _ffn_kernel,
        out_shape=jax.ShapeDtypeStruct((n_tok, d), F32),
        grid=(n_tok // tm, D_FF // tf),
        in_specs=[
            pl.BlockSpec((tm, d), lambda i, j: (i, 0)),
            pl.BlockSpec((1, d), lambda i, j: (0, 0)),
            mspec(3), mspec(4), mspec(5),
            pl.BlockSpec((d, tf), lambda i, j: (0, j)),
            pl.BlockSpec((tf, d), lambda i, j: (j, 0)),
        ],
        out_specs=pl.BlockSpec((tm, d), lambda i, j: (i, 0)),
        scratch_shapes=[pltpu.VMEM((tm, d), BF16)],
        compiler_params=_cparams("arbitrary", "arbitrary"),
        name="ffn",
    )(xall, norm_g.reshape(1, d), mod3, mod3, mod3, w1, w2)


def _final_norm_kernel(x_ref, g_ref, o_ref):
    x = x_ref[...]
    o_ref[...] = x * lax.rsqrt(jnp.mean(x * x, axis=-1, keepdims=True) + EPS) * g_ref[...]


def _final_norm(xall, g, lay):
    d = xall.shape[1]
    tm = lay["tm"]
    first = lay["nc"] // tm
    n_lat = lay["b"] * lay["s"]
    return pl.pallas_call(
        _final_norm_kernel,
        out_shape=jax.ShapeDtypeStruct((n_lat, d), F32),
        grid=(n_lat // tm,),
        in_specs=[pl.BlockSpec((tm, d), lambda i: (i + first, 0)), pl.BlockSpec((1, d), lambda i: (0, 0))],
        out_specs=pl.BlockSpec((tm, d), lambda i: (i, 0)),
        compiler_params=_cparams("arbitrary"),
        name="final_norm",
    )(xall, g.reshape(1, d))


def _reorder_in_columns(w_in):
    q_end = 5 * D_BRANCH
    kv = 2 * N_KV_HEADS * HEAD_DIM
    pad = jnp.zeros(w_in.shape[:-1] + (IN_COLS_PAD - IN_COLS,), w_in.dtype)
    return jnp.concatenate([w_in[..., :q_end], w_in[..., q_end + kv:], w_in[..., q_end:q_end + kv], pad], axis=-1)


def kernel(x, c, ctx, c_ctx, w_mod, b_mod, norm1_g, norm2_g, w_in, gmlp_ln_g, gmlp_ln_b, gmlp_ws, gmlp_bs, conv_w, conv_b, conv_ln_g, conv_ln_b, attn_sink, s5_a_re, s5_a_im, s5_log_step, s5_b_re, s5_b_im, s5_c_re, s5_c_im, s5_d, s5_w_glu, w_branch, w_gate, b_gate, w_out, w_ff1, w_ff2, final_g):
    b, s, d = x.shape
    n_ctx = ctx.shape[1]
    depth = w_mod.shape[0]
    assert d == D_MODEL and s % GRID_W == 0
    nc = b * n_ctx
    tm = _row_tile(nc, s)
    tseq = min(256, tm)
    nct, lpt = nc // tm, s // tm

    def mod_row(i, which):
        bidx = jnp.where(i < nct, b, (i - nct) // lpt)
        return bidx * N_MOD + which

    lay = dict(b=b, s=s, n_ctx=n_ctx, nc=nc, tm=tm, tseq=tseq, mod_row=mod_row)

    cond = jnp.concatenate([c, c_ctx[None, :]], axis=0)
    cond = jnp.pad(cond, ((0, -cond.shape[0] % 8), (0, 0)))
    mod_all = _modulation(cond, w_mod, b_mod)
    rope_tab = _rope_tables(s)
    xall = jnp.concatenate([ctx.reshape(nc, d), x.reshape(b * s, d)], axis=0)

    for l in range(depth):
        mod3 = mod_all[l, :b + 1].reshape((b + 1) * N_MOD, 1, d)
        w_in_l = _reorder_in_columns(w_in[l]).astype(BF16)
        h, z, ud = _in_projection(xall, norm1_g[l], mod3, w_in_l, lay)
        br_a = _gmlp(z, gmlp_ln_g[l], gmlp_ln_b[l], gmlp_ws[l], gmlp_bs[l], lay)
        br_b = _conformer_conv(z, conv_w[l], conv_b[l], conv_ln_g[l], conv_ln_b[l], lay)
        br_c = _attention(z, attn_sink[l], rope_tab, lay)
        tables = _s5_tables(s5_a_re[l], s5_a_im[l], s5_log_step[l], s5_b_re[l], s5_b_im[l], s5_c_re[l], s5_c_im[l])
        y = _s5_mix(ud, tables, s5_d[l], lay)
        br_d = _s5_readout(y, s5_w_glu[l].astype(BF16), lay)
        merged = _merge(h, (br_a, br_b, br_c, br_d), w_gate[l].astype(BF16), b_gate[l], w_branch[l].astype(BF16), lay)
        xall = _out_projection(merged, w_out[l].astype(BF16), xall, mod3, lay)
        xall = _ffn(xall, norm2_g[l], mod3, w_ff1[l].astype(BF16), w_ff2[l].astype(BF16), lay)

    return _final_norm(xall, final_g, lay).reshape(b, s, d)
```

```python
import functools
import math

import jax
import jax.numpy as jnp
from jax import lax
from jax.experimental import pallas as pl
from jax.experimental.pallas import tpu as pltpu

D_MODEL = 2048
N_BRANCH = 4
D_BRANCH = D_MODEL // N_BRANCH
CHUNK = 128
GMLP_GROUPS = 4
CONV_W = 31
CONV_HALO = 16
HEAD_DIM = 64
N_Q_HEADS = D_BRANCH // HEAD_DIM
N_KV_HEADS = 2
Q_PER_KV = N_Q_HEADS // N_KV_HEADS
WINDOW = 128
ROPE_BASE = 10000.0
GRID_W = 64
S5_GW = 16
S5_GROUPS = D_BRANCH // S5_GW
S5_STATE = 64
D_FF = 4 * D_MODEL
N_MOD = 6
EPS = 1e-6
NEG_INF = -1e30

COL_A = 0
COL_B = 2 * D_BRANCH
COL_Q = 4 * D_BRANCH
COL_D = 5 * D_BRANCH
COL_K = 6 * D_BRANCH
COL_V = COL_K + N_KV_HEADS * HEAD_DIM
IN_COLS = COL_V + N_KV_HEADS * HEAD_DIM
IN_COLS_PAD = -(-IN_COLS // D_BRANCH) * D_BRANCH

VMEM_LIMIT_V7X = 56 * 1024 * 1024

F32 = jnp.float32
BF16 = jnp.bfloat16


def _cparams(*sem):
    return pltpu.CompilerParams(dimension_semantics=sem, vmem_limit_bytes=VMEM_LIMIT_V7X)


def _gelu_tanh(x):
    return 0.5 * x * (1.0 + jnp.tanh(math.sqrt(2.0 / math.pi) * (x + 0.044715 * (x * x * x))))


def _sigmoid(x):
    return 1.0 / (1.0 + jnp.exp(-x))


def _layer_norm_rows(v, g, b):
    mu = jnp.mean(v, axis=-1, keepdims=True)
    vc = v - mu
    var = jnp.mean(vc * vc, axis=-1, keepdims=True)
    return vc * lax.rsqrt(var + EPS) * g + b


def _row_tile(nc, s):
    for t in (1024, 512, 256, 128):
        if nc % t == 0 and s % t == 0:
            return t
    raise ValueError("context and sequence lengths must be multiples of 128")


def _mod_kernel(c_ref, w_ref, b_ref, o_ref):
    c = c_ref[...]
    c = (c * _sigmoid(c)).astype(BF16)
    o_ref[0] = jnp.dot(c, w_ref[0].astype(BF16), preferred_element_type=F32) + b_ref[0]


def _modulation(cond, w_mod, b_mod):
    depth, d, n = w_mod.shape
    rows = cond.shape[0]
    tn = 1024
    return pl.pallas_call(
        _mod_kernel,
        out_shape=jax.ShapeDtypeStruct((depth, rows, n), F32),
        grid=(depth, n // tn),
        in_specs=[
            pl.BlockSpec((rows, d), lambda l, j: (0, 0)),
            pl.BlockSpec((1, d, tn), lambda l, j: (l, 0, j)),
            pl.BlockSpec((1, 1, tn), lambda l, j: (l, 0, j)),
        ],
        out_specs=pl.BlockSpec((1, rows, tn), lambda l, j: (l, 0, j)),
        compiler_params=_cparams("arbitrary", "arbitrary"),
        name="modulation",
    )(cond, w_mod, b_mod.reshape(depth, 1, n))


NORM_ROWS = 128


def _modulated_rms_norm(x_ref, g_ref, shift_ref, scale_ref, h_ref):
    gain = g_ref[...] * (1.0 + scale_ref[0])
    shift = shift_ref[0]

    def body(r, carry):
        rows = pl.ds(pl.multiple_of(r * NORM_ROWS, NORM_ROWS), NORM_ROWS)
        x = x_ref[rows, :]
        inv = lax.rsqrt(jnp.mean(x * x, axis=-1, keepdims=True) + EPS)
        h_ref[rows, :] = (x * inv * gain + shift).astype(h_ref.dtype)
        return carry

    lax.fori_loop(0, x_ref.shape[0] // NORM_ROWS, body, 0)


def _inproj_kernel(x_ref, g_ref, shift_ref, scale_ref, w_ref, h_ref, z_ref, ud_ref, *, d_tile):
    j = pl.program_id(1)

    @pl.when(j == 0)
    def _():
        _modulated_rms_norm(x_ref, g_ref, shift_ref, scale_ref, h_ref)

    z = jnp.dot(h_ref[...], w_ref[...], preferred_element_type=F32)
    z_ref[...] = z

    @pl.when(j == d_tile)
    def _():
        ud_ref[...] = z.astype(ud_ref.dtype)


def _in_projection(xall, norm_g, mod3, w_in, layer, lay):
    n_tok, d = xall.shape
    tm, tn = lay["tm"], D_BRANCH
    mod_row = lay["mod_row"]
    return pl.pallas_call(
        functools.partial(_inproj_kernel, d_tile=COL_D // tn),
        out_shape=(jax.ShapeDtypeStruct((n_tok, d), BF16), jax.ShapeDtypeStruct((n_tok, IN_COLS_PAD), F32),
                   jax.ShapeDtypeStruct((n_tok, D_BRANCH), BF16)),
        grid=(n_tok // tm, IN_COLS_PAD // tn),
        in_specs=[
            pl.BlockSpec((tm, d), lambda i, j: (i, 0)),
            pl.BlockSpec((1, d), lambda i, j: (0, 0)),
            pl.BlockSpec((1, 1, d), lambda i, j: (mod_row(i, 0), 0, 0)),
            pl.BlockSpec((1, 1, d), lambda i, j: (mod_row(i, 1), 0, 0)),
            pl.BlockSpec((None, d, tn), lambda i, j: (layer, 0, j)),
        ],
        out_specs=(pl.BlockSpec((tm, d), lambda i, j: (i, 0)), pl.BlockSpec((tm, tn), lambda i, j: (i, j)),
                   pl.BlockSpec((tm, D_BRANCH), lambda i, j: (i, 0))),
        compiler_params=_cparams("arbitrary", "arbitrary"),
        name="in_projection",
    )(xall, norm_g.reshape(1, d), mod3, mod3, w_in)


def _gmlp_kernel(za_ref, lng_ref, lnb_ref, ws_ref, bs_ref, o_ref):
    rows = za_ref.shape[0]
    gw = D_BRANCH // GMLP_GROUPS
    a = _gelu_tanh(za_ref[...])
    u = a[:, :D_BRANCH]
    v = _layer_norm_rows(a[:, D_BRANCH:], lng_ref[...], lnb_ref[...]).astype(BF16)
    for c in range(rows // CHUNK):
        r0 = c * CHUNK
        for g in range(GMLP_GROUPS):
            c0 = g * gw
            mixed = jnp.dot(ws_ref[g], v[r0:r0 + CHUNK, c0:c0 + gw], preferred_element_type=F32) + bs_ref[g]
            o_ref[r0:r0 + CHUNK, c0:c0 + gw] = (u[r0:r0 + CHUNK, c0:c0 + gw] * mixed).astype(o_ref.dtype)


def _gmlp(z, ln_g, ln_b, ws, bs, lay):
    n_tok = z.shape[0]
    rows = lay["tseq"]
    gw = D_BRANCH // GMLP_GROUPS
    bs_full = jnp.broadcast_to(bs[:, :, None], (GMLP_GROUPS, CHUNK, gw)).astype(F32)
    return pl.pallas_call(
        _gmlp_kernel,
        out_shape=jax.ShapeDtypeStruct((n_tok, D_BRANCH), BF16),
        grid=(n_tok // rows,),
        in_specs=[
            pl.BlockSpec((rows, 2 * D_BRANCH), lambda i: (i, COL_A // (2 * D_BRANCH))),
            pl.BlockSpec((1, D_BRANCH), lambda i: (0, 0)),
            pl.BlockSpec((1, D_BRANCH), lambda i: (0, 0)),
            pl.BlockSpec((GMLP_GROUPS, CHUNK, CHUNK), lambda i: (0, 0, 0)),
            pl.BlockSpec((GMLP_GROUPS, CHUNK, gw), lambda i: (0, 0, 0)),
        ],
        out_specs=pl.BlockSpec((rows, D_BRANCH), lambda i: (i, 0)),
        compiler_params=_cparams("arbitrary"),
        name="gmlp_mix",
    )(z, ln_g.reshape(1, -1), ln_b.reshape(1, -1), ws.astype(BF16), bs_full)


def _conv_kernel(zm_ref, zp_ref, zn_ref, w_ref, cb_ref, lng_ref, lnb_ref, o_ref, ybuf, *, seg_tiles):
    rows = zm_ref.shape[0]
    i = pl.program_id(0)
    nct, cpt, lpt = seg_tiles
    pos = jnp.where(i < nct, i % cpt, (i - nct) % lpt)
    seg = jnp.where(i < nct, cpt, lpt)
    keep_prev = (pos != 0).astype(F32)
    keep_next = (pos != seg - 1).astype(F32)

    def glu(ref):
        z = ref[...]
        return z[:, :D_BRANCH] * _sigmoid(z[:, D_BRANCH:])

    ybuf[0:CONV_HALO, :] = glu(zp_ref) * keep_prev
    ybuf[CONV_HALO:CONV_HALO + rows, :] = glu(zm_ref)
    ybuf[CONV_HALO + rows:, :] = glu(zn_ref) * keep_next

    sub, tile = 32, 8
    off = CONV_HALO - CONV_W // 2
    for r in range(rows // sub):
        acc = jnp.zeros((sub, D_BRANCH), F32)
        for res in range(tile):
            part = None
            for k in range(CONV_W):
                if (off + k) % tile != res:
                    continue
                start = r * sub + off + k - res
                term = ybuf[start:start + sub + tile, :] * w_ref[k:k + 1, :]
                part = term if part is None else part + term
            if part is not None:
                acc = acc + part[res:res + sub, :]
        y = _layer_norm_rows(acc + cb_ref[...], lng_ref[...], lnb_ref[...])
        o_ref[r * sub:(r + 1) * sub, :] = (y * _sigmoid(y)).astype(o_ref.dtype)


def _conformer_conv(z, w_dw, b_dw, ln_g, ln_b, lay):
    n_tok = z.shape[0]
    rows = lay["tseq"]
    hb = rows // CONV_HALO
    n_halo_blocks = n_tok // CONV_HALO
    colb = COL_B // (2 * D_BRANCH)
    seg_tiles = (lay["nc"] // rows, lay["n_ctx"] // rows, lay["s"] // rows)
    vec = lambda a: a.reshape(1, -1)
    return pl.pallas_call(
        functools.partial(_conv_kernel, seg_tiles=seg_tiles),
        out_shape=jax.ShapeDtypeStruct((n_tok, D_BRANCH), BF16),
        grid=(n_tok // rows,),
        in_specs=[
            pl.BlockSpec((rows, 2 * D_BRANCH), lambda i: (i, colb)),
            pl.BlockSpec((CONV_HALO, 2 * D_BRANCH), lambda i: (jnp.maximum(i * hb - 1, 0), colb)),
            pl.BlockSpec((CONV_HALO, 2 * D_BRANCH), lambda i: (jnp.minimum((i + 1) * hb, n_halo_blocks - 1), colb)),
            pl.BlockSpec((CONV_W, D_BRANCH), lambda i: (0, 0)),
            pl.BlockSpec((1, D_BRANCH), lambda i: (0, 0)),
            pl.BlockSpec((1, D_BRANCH), lambda i: (0, 0)),
            pl.BlockSpec((1, D_BRANCH), lambda i: (0, 0)),
        ],
        out_specs=pl.BlockSpec((rows, D_BRANCH), lambda i: (i, 0)),
        scratch_shapes=[pltpu.VMEM((rows + 2 * CONV_HALO, D_BRANCH), F32)],
        compiler_params=_cparams("arbitrary"),
        name="conformer_conv",
    )(z, z, z, w_dw, vec(b_dw), vec(ln_g), vec(ln_b))


def _rope(x, cos, sin_a, sin_b):
    reps = x.shape[1] // cos.shape[1]
    if reps > 1:
        cos, sin_a, sin_b = (jnp.concatenate([t] * reps, axis=1) for t in (cos, sin_a, sin_b))
    n = x.shape[1]
    half = HEAD_DIM // 4
    return x * cos + pltpu.roll(x, n - half, 1) * sin_a + pltpu.roll(x, half, 1) * sin_b


def _attn_core(q, keys, vals, bias, sink_ref, o_ref):
    rows = q.shape[0]
    rid = lax.broadcasted_iota(jnp.int32, (Q_PER_KV * rows, 1), 0)
    for h in range(N_KV_HEADS):
        kh = keys[:, h * HEAD_DIM:(h + 1) * HEAD_DIM]
        vh = vals[:, h * HEAD_DIM:(h + 1) * HEAD_DIM]
        heads = [q[:, (h * Q_PER_KV + g) * HEAD_DIM:(h * Q_PER_KV + g + 1) * HEAD_DIM] for g in range(Q_PER_KV)]
        qh = jnp.concatenate(heads, axis=0).astype(BF16)
        s = lax.dot_general(qh, kh, (((1,), (1,)), ((), ())), preferred_element_type=F32)
        if bias is not None:
            s = s + jnp.concatenate([bias] * Q_PER_KV, axis=0)
        sink = jnp.full((Q_PER_KV * rows, 1), sink_ref[h * Q_PER_KV], F32)
        for g in range(1, Q_PER_KV):
            sink = jnp.where(rid >= g * rows, sink_ref[h * Q_PER_KV + g], sink)
        m = jnp.maximum(jnp.max(s, axis=-1, keepdims=True), sink)
        e = jnp.exp(s - m)
        denom = jnp.sum(e, axis=-1, keepdims=True) + jnp.exp(sink - m)
        o = jnp.dot(e.astype(BF16), vh, preferred_element_type=F32) * (1.0 / denom)
        for g in range(Q_PER_KV):
            c0 = (h * Q_PER_KV + g) * HEAD_DIM
            o_ref[:, c0:c0 + HEAD_DIM] = o[g * rows:(g + 1) * rows, :].astype(o_ref.dtype)


def _attn_latent_kernel(sink_ref, q_ref, kp_ref, kc_ref, kn_ref, vp_ref, vc_ref, vn_ref, kx_ref, vx_ref,
                        cq_ref, cp_ref, cn_ref, o_ref, *, n_blocks):
    n = pl.program_id(1)
    rows = q_ref.shape[0]
    scale = HEAD_DIM ** -0.5

    def tables(ref):
        return ref[0], ref[1], ref[2]

    q = _rope(q_ref[...], *tables(cq_ref)) * scale
    kb = [_rope(kp_ref[...], *tables(cp_ref)), _rope(kc_ref[...], *tables(cq_ref)), _rope(kn_ref[...], *tables(cn_ref))]
    keys = jnp.concatenate([k.astype(BF16) for k in kb] + [kx_ref[...].astype(BF16)], axis=0)
    vals = jnp.concatenate([vp_ref[...].astype(BF16), vc_ref[...].astype(BF16), vn_ref[...].astype(BF16),
                            vx_ref[...].astype(BF16)], axis=0)
    nk = keys.shape[0]
    qi = lax.broadcasted_iota(jnp.int32, (rows, nk), 0)
    kj = lax.broadcasted_iota(jnp.int32, (rows, nk), 1)
    rel = kj - qi
    lo = jnp.where(n > 0, 0, rows)
    hi = jnp.where(n < n_blocks - 1, 3 * rows, 2 * rows)
    ok = (rel >= 0) & (rel <= 2 * WINDOW) & (kj >= lo) & (kj < hi)
    ok = ok | (kj >= 3 * rows)
    bias = jnp.where(ok, 0.0, NEG_INF).astype(F32)
    _attn_core(q, keys, vals, bias, sink_ref, o_ref)


def _attn_ctx_kernel(sink_ref, q_ref, kx_ref, vx_ref, latent_out_ref, o_ref):
    del latent_out_ref
    q = q_ref[...] * (HEAD_DIM ** -0.5)
    _attn_core(q, kx_ref[...].astype(BF16), vx_ref[...].astype(BF16), None, sink_ref, o_ref)


def _rope_tables(s):
    t = jnp.arange(s)
    row = (t // GRID_W).astype(F32)[:, None]
    col = (t % GRID_W).astype(F32)[:, None]
    half = HEAD_DIM // 2
    inv = ROPE_BASE ** (-jnp.arange(0, half, 2, dtype=F32) / half)
    lane = jnp.arange(2 * HEAD_DIM)
    d = lane % HEAD_DIM
    freq = inv[(d % half) % (half // 2)][None, :]
    ang = jnp.where((d < half)[None, :], row * freq, col * freq)
    first = ((d % half) < half // 2)[None, :]
    sin = jnp.sin(ang)
    return jnp.stack([jnp.cos(ang), jnp.where(first, -sin, 0.0), jnp.where(first, 0.0, sin)]).astype(F32)


def _attention(z, sink, rope_tab, lay):
    n_tok = z.shape[0]
    b, s, n_ctx, nc = lay["b"], lay["s"], lay["n_ctx"], lay["nc"]
    nb = s // CHUNK
    cb0 = nc // CHUNK
    kvw = N_KV_HEADS * HEAD_DIM
    qcol, kcol, vcol = COL_Q // D_BRANCH, COL_K // kvw, COL_V // kvw
    lat = lambda bb, n: cb0 + bb * nb + n
    prev = lambda n: jnp.maximum(n - 1, 0)
    nxt = lambda n: jnp.minimum(n + 1, nb - 1)
    smem = pl.BlockSpec(memory_space=pltpu.SMEM)
    out_lat = pl.pallas_call(
        functools.partial(_attn_latent_kernel, n_blocks=nb),
        out_shape=jax.ShapeDtypeStruct((n_tok, D_BRANCH), BF16),
        grid=(b, nb),
        in_specs=[
            smem,
            pl.BlockSpec((CHUNK, D_BRANCH), lambda bb, n: (lat(bb, n), qcol)),
            pl.BlockSpec((CHUNK, kvw), lambda bb, n: (lat(bb, prev(n)), kcol)),
            pl.BlockSpec((CHUNK, kvw), lambda bb, n: (lat(bb, n), kcol)),
            pl.BlockSpec((CHUNK, kvw), lambda bb, n: (lat(bb, nxt(n)), kcol)),
            pl.BlockSpec((CHUNK, kvw), lambda bb, n: (lat(bb, prev(n)), vcol)),
            pl.BlockSpec((CHUNK, kvw), lambda bb, n: (lat(bb, n), vcol)),
            pl.BlockSpec((CHUNK, kvw), lambda bb, n: (lat(bb, nxt(n)), vcol)),
            pl.BlockSpec((n_ctx, kvw), lambda bb, n: (bb, kcol)),
            pl.BlockSpec((n_ctx, kvw), lambda bb, n: (bb, vcol)),
            pl.BlockSpec((3, CHUNK, kvw), lambda bb, n: (0, n, 0)),
            pl.BlockSpec((3, CHUNK, kvw), lambda bb, n: (0, prev(n), 0)),
            pl.BlockSpec((3, CHUNK, kvw), lambda bb, n: (0, nxt(n), 0)),
        ],
        out_specs=pl.BlockSpec((CHUNK, D_BRANCH), lambda bb, n: (lat(bb, n), 0)),
        compiler_params=_cparams("arbitrary", "arbitrary"),
        name="attention_latent",
    )(sink, z, z, z, z, z, z, z, z, z, rope_tab, rope_tab, rope_tab)
    ncb = n_ctx // CHUNK
    return pl.pallas_call(
        _attn_ctx_kernel,
        out_shape=jax.ShapeDtypeStruct((n_tok, D_BRANCH), BF16),
        grid=(b, ncb),
        in_specs=[
            smem,
            pl.BlockSpec((CHUNK, D_BRANCH), lambda bb, n: (bb * ncb + n, qcol)),
            pl.BlockSpec((n_ctx, kvw), lambda bb, n: (bb, kcol)),
            pl.BlockSpec((n_ctx, kvw), lambda bb, n: (bb, vcol)),
            pl.BlockSpec(memory_space=pl.ANY),
        ],
        out_specs=pl.BlockSpec((CHUNK, D_BRANCH), lambda bb, n: (bb * ncb + n, 0)),
        input_output_aliases={4: 0},
        compiler_params=_cparams("arbitrary", "arbitrary"),
        name="attention_ctx",
    )(sink, z, z, z, out_lat)


def _s5_tables(a_re, a_im, log_step, b_re, b_im, c_re, c_im):
    t_len = CHUNK
    hi = lax.Precision.HIGHEST
    j = jnp.arange(t_len + 1, dtype=F32)
    lam = lax.complex(a_re.astype(F32), a_im.astype(F32))
    log_lbar = lam * jnp.exp(log_step.astype(F32))[..., None]
    lbar = jnp.exp(log_lbar)
    b = lax.complex(b_re.astype(F32), b_im.astype(F32))
    bbar = ((lbar - 1.0) / lam)[..., None] * b[None]
    cc = lax.complex(c_re.astype(F32), c_im.astype(F32))
    pw = jnp.exp(log_lbar[:, :, None, :] * j[None, None, :, None])
    pw_re, pw_im = pw.real, pw.imag

    q = cc[:, :, :, None, :] * bbar.transpose(0, 1, 3, 2)[:, :, None, :, :]
    taps = (jnp.einsum("dgcip,dgjp->dgjci", q.real, pw_re[:, :, :t_len], precision=hi)
            - jnp.einsum("dgcip,dgjp->dgjci", q.imag, pw_im[:, :, :t_len], precision=hi))
    kf, kb = taps[0], taps[1]
    kc = jnp.concatenate([kb[:, :0:-1], (kf[:, :1] + kb[:, :1]), kf[:, 1:]], axis=1)
    strip = kc.transpose(0, 3, 1, 2).reshape(S5_GROUPS, S5_GW, (2 * t_len - 1) * S5_GW)
    width = 2 * t_len * S5_GW
    strip = jnp.pad(strip, ((0, 0), (0, 0), (0, width + 128 - strip.shape[-1]))).astype(BF16)
    strips = jnp.stack([strip[:, :, r * S5_GW:r * S5_GW + width] for r in range(8)], axis=1)

    def outer_in(p_re, p_im, d):
        bt = bbar[d].transpose(0, 2, 1)[:, None]
        pr, pi = p_re[:, :, None, :], p_im[:, :, None, :]
        return pr * bt.real - pi * bt.imag, pr * bt.imag + pi * bt.real

    f_re, f_im = outer_in(pw_re[0, :, t_len - 1::-1], pw_im[0, :, t_len - 1::-1], 0)
    b_re_, b_im_ = outer_in(pw_re[1, :, :t_len], pw_im[1, :, :t_len], 1)
    wst = jnp.stack([f_re, b_re_, f_im, b_im_], axis=3).astype(BF16).reshape(
        S5_GROUPS, t_len * S5_GW, 4 * S5_STATE)

    def outer_out(p_re, p_im, d):
        ct = cc[d].transpose(0, 2, 1)[:, :, None, :]
        pr, pi = p_re.transpose(0, 2, 1)[..., None], p_im.transpose(0, 2, 1)[..., None]
        return ct.real * pr - ct.imag * pi, -(ct.real * pi + ct.imag * pr)

    cf_re, cf_im = outer_out(pw_re[0, :, 1:t_len + 1], pw_im[0, :, 1:t_len + 1], 0)
    cb_re, cb_im = outer_out(pw_re[1, :, t_len:0:-1], pw_im[1, :, t_len:0:-1], 1)
    cl = jnp.stack([cf_re, cb_re, cf_im, cb_im], axis=1).astype(BF16).reshape(
        S5_GROUPS, 4 * S5_STATE, t_len * S5_GW)

    dre = jnp.concatenate([pw_re[0, :, t_len], pw_re[1, :, t_len]], axis=-1)
    dim = jnp.concatenate([pw_im[0, :, t_len], pw_im[1, :, t_len]], axis=-1)
    decay = jnp.stack([dre, dim], axis=1)
    return strips, wst, cl, decay.astype(F32)


def _s5_kernel(u_ref, strips_ref, wst_ref, cl_ref, dec_ref, dskip_ref, y_ref,
               mt_ref, dsre_ref, dsim_ref, fre_ref, fim_ref, bre_ref, bim_ref, *, batch, n_ctx_steps, n_lat_steps):
    t_len = CHUNK
    width = t_len * S5_GW
    lanes = 2 * S5_STATE
    for s in range(t_len):
        off = (t_len - 1 - s) * S5_GW
        r, q = (off // S5_GW) % 8, off // 128
        mt_ref[s * S5_GW:(s + 1) * S5_GW, :] = strips_ref[0, r, :, q * 128:q * 128 + width]
    u = u_ref[0]
    ds = jnp.dot(u, wst_ref[0], preferred_element_type=F32)
    dsre_ref[...] = ds[:, :lanes]
    dsim_ref[...] = ds[:, lanes:]

    dre = jnp.broadcast_to(dec_ref[0, 0:1, :], (batch, lanes))
    dim = jnp.broadcast_to(dec_ref[0, 1:2, :], (batch, lanes))
    is_fwd = lax.broadcasted_iota(jnp.int32, (batch, lanes), 1) < S5_STATE

    def run(first_row, n_steps, carry):
        def step(it, carry):
            sre, sim = carry
            rows_f = pl.ds(first_row + it, batch, stride=n_steps)
            rows_b = pl.ds(first_row + n_steps - 1 - it, batch, stride=n_steps)
            fre_ref[rows_f, :] = sre
            fim_ref[rows_f, :] = sim
            bre_ref[rows_b, :] = sre
            bim_ref[rows_b, :] = sim
            add_re = jnp.where(is_fwd, dsre_ref[rows_f, :], dsre_ref[rows_b, :])
            add_im = jnp.where(is_fwd, dsim_ref[rows_f, :], dsim_ref[rows_b, :])
            return sre * dre - sim * dim + add_re, sre * dim + sim * dre + add_im

        return lax.fori_loop(0, n_steps, step, carry)

    zero = jnp.zeros((batch, lanes), F32)
    carry = run(0, n_ctx_steps, (zero, zero))
    run(batch * n_ctx_steps, n_lat_steps, carry)

    n_rows = dsre_ref.shape[0]
    fwd_rows = lax.broadcasted_iota(jnp.int32, (n_rows, lanes), 1) < S5_STATE
    sin = jnp.concatenate([jnp.where(fwd_rows, fre_ref[...], bre_ref[...]),
                           jnp.where(fwd_rows, fim_ref[...], bim_ref[...])], axis=1).astype(BF16)
    y = jnp.dot(u, mt_ref[...], preferred_element_type=F32)
    y = y + jnp.dot(sin, cl_ref[0], preferred_element_type=F32)
    y_ref[0] = _gelu_tanh(y + dskip_ref[0] * u.astype(F32)).astype(y_ref.dtype)


def _s5_mix(ud, tables, d_skip, lay):
    b, s, n_ctx = lay["b"], lay["s"], lay["n_ctx"]
    strips, wst, cl, decay = tables
    width = CHUNK * S5_GW
    ncs, nls = n_ctx // CHUNK, s // CHUNK
    m = b * (ncs + nls)
    ug = ud.reshape(m, CHUNK, S5_GROUPS, S5_GW).transpose(2, 0, 1, 3).reshape(S5_GROUPS, m, width)
    dskip = jnp.broadcast_to(d_skip.astype(F32).reshape(S5_GROUPS, 1, 1, S5_GW),
                             (S5_GROUPS, 1, CHUNK, S5_GW)).reshape(S5_GROUPS, 1, width)
    state_rows = pltpu.VMEM((m, 2 * S5_STATE), F32)
    y = pl.pallas_call(
        functools.partial(_s5_kernel, batch=b, n_ctx_steps=ncs, n_lat_steps=nls),
        out_shape=jax.ShapeDtypeStruct((S5_GROUPS, m, width), BF16),
        grid=(S5_GROUPS,),
        in_specs=[
            pl.BlockSpec((1, m, width), lambda g: (g, 0, 0)),
            pl.BlockSpec((1, 8, S5_GW, 2 * width), lambda g: (g, 0, 0, 0)),
            pl.BlockSpec((1, width, 4 * S5_STATE), lambda g: (g, 0, 0)),
            pl.BlockSpec((1, 4 * S5_STATE, width), lambda g: (g, 0, 0)),
            pl.BlockSpec((1, 2, 2 * S5_STATE), lambda g: (g, 0, 0)),
            pl.BlockSpec((1, 1, width), lambda g: (g, 0, 0)),
        ],
        out_specs=pl.BlockSpec((1, m, width), lambda g: (g, 0, 0)),
        scratch_shapes=[
            pltpu.VMEM((width, width), BF16),
            state_rows, state_rows, state_rows, state_rows, state_rows, state_rows,
        ],
        compiler_params=_cparams("arbitrary"),
        name="s5_scan",
    )(ug, strips, wst, cl, decay, dskip)
    return y.reshape(S5_GROUPS, m, CHUNK, S5_GW).transpose(1, 2, 0, 3).reshape(m * CHUNK, D_BRANCH)


def _s5_readout_kernel(y_ref, w_ref, o_ref):
    r = jnp.dot(y_ref[...], w_ref[...], preferred_element_type=F32)
    o_ref[...] = (r[:, :D_BRANCH] * _sigmoid(r[:, D_BRANCH:])).astype(o_ref.dtype)


def _s5_readout(y, w_glu, layer, lay):
    n_tok = y.shape[0]
    rows = lay["tm"]
    return pl.pallas_call(
        _s5_readout_kernel,
        out_shape=jax.ShapeDtypeStruct((n_tok, D_BRANCH), BF16),
        grid=(n_tok // rows,),
        in_specs=[
            pl.BlockSpec((rows, D_BRANCH), lambda i: (i, 0)),
            pl.BlockSpec((None, D_BRANCH, 2 * D_BRANCH), lambda i: (layer, 0, 0)),
        ],
        out_specs=pl.BlockSpec((rows, D_BRANCH), lambda i: (i, 0)),
        compiler_params=_cparams("arbitrary"),
        name="s5_readout",
    )(y, w_glu)


def _merge_kernel(h_ref, ba_ref, bb_ref, bc_ref, bd_ref, wg_ref, bg_ref, wb_ref, o_ref):
    h = h_ref[...]
    acc = None
    for k, br_ref in enumerate((ba_ref, bb_ref, bc_ref, bd_ref)):
        gate = _sigmoid(jnp.dot(h, wg_ref[k], preferred_element_type=F32) + bg_ref[k])
        term = gate * jnp.dot(br_ref[...], wb_ref[k], preferred_element_type=F32)
        acc = term if acc is None else acc + term
    o_ref[...] = acc.astype(o_ref.dtype)


def _merge(h, branches, w_gate, b_gate, w_branch, layer, first, lay):
    n_tok, d = h.shape
    tm, tn = lay["tm"], 512
    br_spec = pl.BlockSpec((tm, D_BRANCH), lambda i, j: (i + first, 0))
    return pl.pallas_call(
        _merge_kernel,
        out_shape=jax.ShapeDtypeStruct((n_tok, d), BF16),
        grid=(n_tok // tm - first, d // tn),
        in_specs=[
            pl.BlockSpec((tm, d), lambda i, j: (i + first, 0)),
            br_spec, br_spec, br_spec, br_spec,
            pl.BlockSpec((None, N_BRANCH, d, tn), lambda i, j: (layer, 0, 0, j)),
            pl.BlockSpec((N_BRANCH, 1, tn), lambda i, j: (0, 0, j)),
            pl.BlockSpec((None, N_BRANCH, D_BRANCH, tn), lambda i, j: (layer, 0, 0, j)),
        ],
        out_specs=pl.BlockSpec((tm, tn), lambda i, j: (i + first, j)),
        compiler_params=_cparams("arbitrary", "arbitrary"),
        name="branch_merge",
    )(h, *branches, w_gate, b_gate.reshape(N_BRANCH, 1, d), w_branch)


def _outproj_kernel(m_ref, w_ref, x_ref, gate_ref, o_ref):
    o_ref[...] = x_ref[...] + gate_ref[0] * jnp.dot(m_ref[...], w_ref[...], preferred_element_type=F32)


def _out_projection(merged, w_out, xall, mod3, layer, first, lay):
    n_tok, d = xall.shape
    tm, tn = lay["tm"], 512
    mod_row = lay["mod_row"]
    return pl.pallas_call(
        _outproj_kernel,
        out_shape=jax.ShapeDtypeStruct((n_tok, d), F32),
        grid=(n_tok // tm - first, d // tn),
        in_specs=[
            pl.BlockSpec((tm, d), lambda i, j: (i + first, 0)),
            pl.BlockSpec((None, d, tn), lambda i, j: (layer, 0, j)),
            pl.BlockSpec((tm, tn), lambda i, j: (i + first, j)),
            pl.BlockSpec((1, 1, tn), lambda i, j: (mod_row(i + first, 2), 0, j)),
        ],
        out_specs=pl.BlockSpec((tm, tn), lambda i, j: (i + first, j)),
        compiler_params=_cparams("arbitrary", "arbitrary"),
        name="out_projection",
    )(merged, w_out, xall, mod3)


def _ffn_kernel(x_ref, g_ref, shift_ref, scale_ref, gate_ref, w1_ref, w2_ref, o_ref, h_ref):
    j = pl.program_id(1)

    @pl.when(j == 0)
    def _():
        _modulated_rms_norm(x_ref, g_ref, shift_ref, scale_ref, h_ref)
        o_ref[...] = jnp.zeros_like(o_ref)

    a = jnp.maximum(jnp.dot(h_ref[...], w1_ref[...], preferred_element_type=F32), 0.0)
    o_ref[...] += jnp.dot((a * a).astype(BF16), w2_ref[...], preferred_element_type=F32)

    @pl.when(j == pl.num_programs(1) - 1)
    def _():
        o_ref[...] = x_ref[...] + gate_ref[0] * o_ref[...]


def _ffn(xall, norm_g, mod3, w1, w2, layer, first, lay):
    n_tok, d = xall.shape
    tm, tf = lay["tm"], 512
    mod_row = lay["mod_row"]
    mspec = lambda which: pl.BlockSpec((1, 1, d), lambda i, j: (mod_row(i + first, which), 0, 0))
    return pl.pallas_call(
        _ffn_kernel,
        out_shape=jax.ShapeDtypeStruct((n_tok, d), F32),
        grid=(n_tok // tm - first, D_FF // tf),
        in_specs=[
            pl.BlockSpec((tm, d), lambda i, j: (i + first, 0)),
            pl.BlockSpec((1, d), lambda i, j: (0, 0)),
            mspec(3), mspec(4), mspec(5),
            pl.BlockSpec((None, d, tf), lambda i, j: (layer, 0, j)),
            pl.BlockSpec((None, tf, d), lambda i, j: (layer, j, 0)),
        ],
        out_specs=pl.BlockSpec((tm, d), lambda i, j: (i + first, 0)),
        scratch_shapes=[pltpu.VMEM((tm, d), BF16)],
        compiler_params=_cparams("arbitrary", "arbitrary"),
        name="ffn",
    )(xall, norm_g.reshape(1, d), mod3, mod3, mod3, w1, w2)


def _final_norm_kernel(x_ref, g_ref, o_ref):
    x = x_ref[...]
    o_ref[...] = x * lax.rsqrt(jnp.mean(x * x, axis=-1, keepdims=True) + EPS) * g_ref[...]


def _final_norm(xall, g, lay):
    d = xall.shape[1]
    tm = lay["tm"]
    first = lay["nc"] // tm
    n_lat = lay["b"] * lay["s"]
    return pl.pallas_call(
        _final_norm_kernel,
        out_shape=jax.ShapeDtypeStruct((n_lat, d), F32),
        grid=(n_lat // tm,),
        in_specs=[pl.BlockSpec((tm, d), lambda i: (i + first, 0)), pl.BlockSpec((1, d), lambda i: (0, 0))],
        out_specs=pl.BlockSpec((tm, d), lambda i: (i, 0)),
        compiler_params=_cparams("arbitrary"),
        name="final_norm",
    )(xall, g.reshape(1, d))


def _reorder_in_columns(w_in):
    q_end = 5 * D_BRANCH
    kv = 2 * N_KV_HEADS * HEAD_DIM
    pad = jnp.zeros(w_in.shape[:-1] + (IN_COLS_PAD - IN_COLS,), w_in.dtype)
    return jnp.concatenate([w_in[..., :q_end], w_in[..., q_end + kv:], w_in[..., q_end:q_end + kv], pad], axis=-1)


def kernel(x, c, ctx, c_ctx, w_mod, b_mod, norm1_g, norm2_g, w_in, gmlp_ln_g, gmlp_ln_b, gmlp_ws, gmlp_bs, conv_w, conv_b, conv_ln_g, conv_ln_b, attn_sink, s5_a_re, s5_a_im, s5_log_step, s5_b_re, s5_b_im, s5_c_re, s5_c_im, s5_d, s5_w_glu, w_branch, w_gate, b_gate, w_out, w_ff1, w_ff2, final_g):
    b, s, d = x.shape
    n_ctx = ctx.shape[1]
    depth = w_mod.shape[0]
    assert d == D_MODEL and s % GRID_W == 0
    nc = b * n_ctx
    tm = _row_tile(nc, s)
    tseq = min(256, tm)
    nct, lpt = nc // tm, s // tm

    def mod_row(i, which):
        bidx = jnp.where(i < nct, b, (i - nct) // lpt)
        return bidx * N_MOD + which

    lay = dict(b=b, s=s, n_ctx=n_ctx, nc=nc, tm=tm, tseq=tseq, mod_row=mod_row)

    cond = jnp.concatenate([c, c_ctx[None, :]], axis=0)
    cond = jnp.pad(cond, ((0, -cond.shape[0] % 8), (0, 0)))
    mod_all = _modulation(cond, w_mod, b_mod)
    rope_tab = _rope_tables(s)
    xall = jnp.concatenate([ctx.reshape(nc, d), x.reshape(b * s, d)], axis=0)

    w_in_b = _reorder_in_columns(w_in).astype(BF16)
    w_gate_b, w_branch_b, w_out_b = w_gate.astype(BF16), w_branch.astype(BF16), w_out.astype(BF16)
    w_ff1_b, w_ff2_b, w_glu_b = w_ff1.astype(BF16), w_ff2.astype(BF16), s5_w_glu.astype(BF16)

    for l in range(depth):
        first = nct if l == depth - 1 else 0
        mod3 = mod_all[l, :b + 1].reshape((b + 1) * N_MOD, 1, d)
        h, z, ud = _in_projection(xall, norm1_g[l], mod3, w_in_b, l, lay)
        br_a = _gmlp(z, gmlp_ln_g[l], gmlp_ln_b[l], gmlp_ws[l], gmlp_bs[l], lay)
        br_b = _conformer_conv(z, conv_w[l], conv_b[l], conv_ln_g[l], conv_ln_b[l], lay)
        br_c = _attention(z, attn_sink[l], rope_tab, lay)
        tables = _s5_tables(s5_a_re[l], s5_a_im[l], s5_log_step[l], s5_b_re[l], s5_b_im[l], s5_c_re[l], s5_c_im[l])
        y = _s5_mix(ud, tables, s5_d[l], lay)
        br_d = _s5_readout(y, w_glu_b, l, lay)
        merged = _merge(h, (br_a, br_b, br_c, br_d), w_gate_b, b_gate[l], w_branch_b, l, first, lay)
        xall = _out_projection(merged, w_out_b, xall, mod3, l, first, lay)
        xall = _ffn(xall, norm2_g[l], mod3, w_ff1_b, w_ff2_b, l, first, lay)

    return _final_norm(xall, final_g, lay).reshape(b, s, d)
```

```python
import functools
import math

import jax
import jax.numpy as jnp
from jax import lax
from jax.experimental import pallas as pl
from jax.experimental.pallas import tpu as pltpu

D_MODEL = 2048
N_BRANCH = 4
D_BRANCH = D_MODEL // N_BRANCH
CHUNK = 128
GMLP_GROUPS = 4
CONV_W = 31
CONV_HALO = 16
HEAD_DIM = 64
N_Q_HEADS = D_BRANCH // HEAD_DIM
N_KV_HEADS = 2
Q_PER_KV = N_Q_HEADS // N_KV_HEADS
WINDOW = 128
ROPE_BASE = 10000.0
GRID_W = 64
S5_GW = 16
S5_GROUPS = D_BRANCH // S5_GW
S5_STATE = 64
D_FF = 4 * D_MODEL
N_MOD = 6
EPS = 1e-6
NEG_INF = -1e30

COL_A = 0
COL_B = 2 * D_BRANCH
COL_Q = 4 * D_BRANCH
COL_D = 5 * D_BRANCH
COL_K = 6 * D_BRANCH
COL_V = COL_K + N_KV_HEADS * HEAD_DIM
IN_COLS = COL_V + N_KV_HEADS * HEAD_DIM
IN_COLS_PAD = -(-IN_COLS // D_BRANCH) * D_BRANCH

VMEM_LIMIT_V7X = 56 * 1024 * 1024

F32 = jnp.float32
BF16 = jnp.bfloat16


def _cparams(*sem):
    return pltpu.CompilerParams(dimension_semantics=sem, vmem_limit_bytes=VMEM_LIMIT_V7X)


def _gelu_tanh(x):
    return 0.5 * x * (1.0 + jnp.tanh(math.sqrt(2.0 / math.pi) * (x + 0.044715 * (x * x * x))))


def _sigmoid(x):
    return 1.0 / (1.0 + jnp.exp(-x))


def _layer_norm_rows(v, g, b):
    mu = jnp.mean(v, axis=-1, keepdims=True)
    vc = v - mu
    var = jnp.mean(vc * vc, axis=-1, keepdims=True)
    return vc * lax.rsqrt(var + EPS) * g + b


def _row_tile(nc, s):
    for t in (1024, 512, 256, 128):
        if nc % t == 0 and s % t == 0:
            return t
    raise ValueError("context and sequence lengths must be multiples of 128")


def _mod_kernel(c_ref, w_ref, b_ref, o_ref):
    c = c_ref[...]
    c = (c * _sigmoid(c)).astype(BF16)
    o_ref[0] = jnp.dot(c, w_ref[0].astype(BF16), preferred_element_type=F32) + b_ref[0]


def _modulation(cond, w_mod, b_mod):
    depth, d, n = w_mod.shape
    rows = cond.shape[0]
    tn = 1024
    return pl.pallas_call(
        _mod_kernel,
        out_shape=jax.ShapeDtypeStruct((depth, rows, n), F32),
        grid=(depth, n // tn),
        in_specs=[
            pl.BlockSpec((rows, d), lambda l, j: (0, 0)),
            pl.BlockSpec((1, d, tn), lambda l, j: (l, 0, j)),
            pl.BlockSpec((1, 1, tn), lambda l, j: (l, 0, j)),
        ],
        out_specs=pl.BlockSpec((1, rows, tn), lambda l, j: (l, 0, j)),
        compiler_params=_cparams("arbitrary", "arbitrary"),
        name="modulation",
    )(cond, w_mod, b_mod.reshape(depth, 1, n))


NORM_ROWS = 128


def _modulated_rms_norm(x_ref, g_ref, shift_ref, scale_ref, h_ref):
    gain = g_ref[...] * (1.0 + scale_ref[0])
    shift = shift_ref[0]

    def body(r, carry):
        rows = pl.ds(pl.multiple_of(r * NORM_ROWS, NORM_ROWS), NORM_ROWS)
        x = x_ref[rows, :]
        inv = lax.rsqrt(jnp.mean(x * x, axis=-1, keepdims=True) + EPS)
        h_ref[rows, :] = (x * inv * gain + shift).astype(h_ref.dtype)
        return carry

    lax.fori_loop(0, x_ref.shape[0] // NORM_ROWS, body, 0)


def _inproj_kernel(x_ref, g_ref, shift_ref, scale_ref, w_ref, h_ref, z_ref, ud_ref, *, d_tile):
    j = pl.program_id(1)

    @pl.when(j == 0)
    def _():
        _modulated_rms_norm(x_ref, g_ref, shift_ref, scale_ref, h_ref)

    z = jnp.dot(h_ref[...], w_ref[...], preferred_element_type=F32)
    z_ref[...] = z

    @pl.when(j == d_tile)
    def _():
        ud_ref[...] = z.astype(ud_ref.dtype)


def _in_projection(xall, norm_g, mod3, w_in, layer, lay):
    n_tok, d = xall.shape
    tm, tn = lay["tm"], D_BRANCH
    mod_row = lay["mod_row"]
    return pl.pallas_call(
        functools.partial(_inproj_kernel, d_tile=COL_D // tn),
        out_shape=(jax.ShapeDtypeStruct((n_tok, d), BF16), jax.ShapeDtypeStruct((n_tok, IN_COLS_PAD), F32),
                   jax.ShapeDtypeStruct((n_tok, D_BRANCH), BF16)),
        grid=(n_tok // tm, IN_COLS_PAD // tn),
        in_specs=[
            pl.BlockSpec((tm, d), lambda i, j: (i, 0)),
            pl.BlockSpec((1, d), lambda i, j: (0, 0)),
            pl.BlockSpec((1, 1, d), lambda i, j: (mod_row(i, 0), 0, 0)),
            pl.BlockSpec((1, 1, d), lambda i, j: (mod_row(i, 1), 0, 0)),
            pl.BlockSpec((None, d, tn), lambda i, j: (layer, 0, j)),
        ],
        out_specs=(pl.BlockSpec((tm, d), lambda i, j: (i, 0)), pl.BlockSpec((tm, tn), lambda i, j: (i, j)),
                   pl.BlockSpec((tm, D_BRANCH), lambda i, j: (i, 0))),
        compiler_params=_cparams("arbitrary", "arbitrary"),
        name="in_projection",
    )(xall, norm_g.reshape(1, d), mod3, mod3, w_in)


def _gmlp_kernel(za_ref, lng_ref, lnb_ref, ws_ref, bs_ref, o_ref):
    rows = za_ref.shape[0]
    gw = D_BRANCH // GMLP_GROUPS
    a = _gelu_tanh(za_ref[...])
    u = a[:, :D_BRANCH]
    v = _layer_norm_rows(a[:, D_BRANCH:], lng_ref[...], lnb_ref[...]).astype(BF16)
    for c in range(rows // CHUNK):
        r0 = c * CHUNK
        for g in range(GMLP_GROUPS):
            c0 = g * gw
            mixed = jnp.dot(ws_ref[g], v[r0:r0 + CHUNK, c0:c0 + gw], preferred_element_type=F32) + bs_ref[g]
            o_ref[r0:r0 + CHUNK, c0:c0 + gw] = (u[r0:r0 + CHUNK, c0:c0 + gw] * mixed).astype(o_ref.dtype)


def _gmlp(z, ln_g, ln_b, ws, bs, lay):
    n_tok = z.shape[0]
    rows = lay["tseq"]
    gw = D_BRANCH // GMLP_GROUPS
    bs_full = jnp.broadcast_to(bs[:, :, None], (GMLP_GROUPS, CHUNK, gw)).astype(F32)
    return pl.pallas_call(
        _gmlp_kernel,
        out_shape=jax.ShapeDtypeStruct((n_tok, D_BRANCH), BF16),
        grid=(n_tok // rows,),
        in_specs=[
            pl.BlockSpec((rows, 2 * D_BRANCH), lambda i: (i, COL_A // (2 * D_BRANCH))),
            pl.BlockSpec((1, D_BRANCH), lambda i: (0, 0)),
            pl.BlockSpec((1, D_BRANCH), lambda i: (0, 0)),
            pl.BlockSpec((GMLP_GROUPS, CHUNK, CHUNK), lambda i: (0, 0, 0)),
            pl.BlockSpec((GMLP_GROUPS, CHUNK, gw), lambda i: (0, 0, 0)),
        ],
        out_specs=pl.BlockSpec((rows, D_BRANCH), lambda i: (i, 0)),
        compiler_params=_cparams("arbitrary"),
        name="gmlp_mix",
    )(z, ln_g.reshape(1, -1), ln_b.reshape(1, -1), ws.astype(BF16), bs_full)


def _conv_kernel(zm_ref, zp_ref, zn_ref, w_ref, cb_ref, lng_ref, lnb_ref, o_ref, ybuf, *, seg_tiles):
    rows = zm_ref.shape[0]
    i = pl.program_id(0)
    nct, cpt, lpt = seg_tiles
    pos = jnp.where(i < nct, i % cpt, (i - nct) % lpt)
    seg = jnp.where(i < nct, cpt, lpt)
    keep_prev = (pos != 0).astype(F32)
    keep_next = (pos != seg - 1).astype(F32)

    def glu(ref):
        z = ref[...]
        return z[:, :D_BRANCH] * _sigmoid(z[:, D_BRANCH:])

    ybuf[0:CONV_HALO, :] = glu(zp_ref) * keep_prev
    ybuf[CONV_HALO:CONV_HALO + rows, :] = glu(zm_ref)
    ybuf[CONV_HALO + rows:, :] = glu(zn_ref) * keep_next

    sub, tile = 32, 8
    off = CONV_HALO - CONV_W // 2
    for r in range(rows // sub):
        acc = jnp.zeros((sub, D_BRANCH), F32)
        for res in range(tile):
            part = None
            for k in range(CONV_W):
                if (off + k) % tile != res:
                    continue
                start = r * sub + off + k - res
                term = ybuf[start:start + sub + tile, :] * w_ref[k:k + 1, :]
                part = term if part is None else part + term
            if part is not None:
                acc = acc + part[res:res + sub, :]
        y = _layer_norm_rows(acc + cb_ref[...], lng_ref[...], lnb_ref[...])
        o_ref[r * sub:(r + 1) * sub, :] = (y * _sigmoid(y)).astype(o_ref.dtype)


def _conformer_conv(z, w_dw, b_dw, ln_g, ln_b, lay):
    n_tok = z.shape[0]
    rows = lay["tseq"]
    hb = rows // CONV_HALO
    n_halo_blocks = n_tok // CONV_HALO
    colb = COL_B // (2 * D_BRANCH)
    seg_tiles = (lay["nc"] // rows, lay["n_ctx"] // rows, lay["s"] // rows)
    vec = lambda a: a.reshape(1, -1)
    return pl.pallas_call(
        functools.partial(_conv_kernel, seg_tiles=seg_tiles),
        out_shape=jax.ShapeDtypeStruct((n_tok, D_BRANCH), BF16),
        grid=(n_tok // rows,),
        in_specs=[
            pl.BlockSpec((rows, 2 * D_BRANCH), lambda i: (i, colb)),
            pl.BlockSpec((CONV_HALO, 2 * D_BRANCH), lambda i: (jnp.maximum(i * hb - 1, 0), colb)),
            pl.BlockSpec((CONV_HALO, 2 * D_BRANCH), lambda i: (jnp.minimum((i + 1) * hb, n_halo_blocks - 1), colb)),
            pl.BlockSpec((CONV_W, D_BRANCH), lambda i: (0, 0)),
            pl.BlockSpec((1, D_BRANCH), lambda i: (0, 0)),
            pl.BlockSpec((1, D_BRANCH), lambda i: (0, 0)),
            pl.BlockSpec((1, D_BRANCH), lambda i: (0, 0)),
        ],
        out_specs=pl.BlockSpec((rows, D_BRANCH), lambda i: (i, 0)),
        scratch_shapes=[pltpu.VMEM((rows + 2 * CONV_HALO, D_BRANCH), F32)],
        compiler_params=_cparams("arbitrary"),
        name="conformer_conv",
    )(z, z, z, w_dw, vec(b_dw), vec(ln_g), vec(ln_b))


def _rope(x, cos, sin_a, sin_b):
    reps = x.shape[1] // cos.shape[1]
    if reps > 1:
        cos, sin_a, sin_b = (jnp.concatenate([t] * reps, axis=1) for t in (cos, sin_a, sin_b))
    n = x.shape[1]
    half = HEAD_DIM // 4
    return x * cos + pltpu.roll(x, n - half, 1) * sin_a + pltpu.roll(x, half, 1) * sin_b


def _attn_core(q, keys, vals, bias, sink_ref, o_ref):
    rows = q.shape[0]
    rid = lax.broadcasted_iota(jnp.int32, (Q_PER_KV * rows, 1), 0)
    for h in range(N_KV_HEADS):
        kh = keys[:, h * HEAD_DIM:(h + 1) * HEAD_DIM]
        vh = vals[:, h * HEAD_DIM:(h + 1) * HEAD_DIM]
        heads = [q[:, (h * Q_PER_KV + g) * HEAD_DIM:(h * Q_PER_KV + g + 1) * HEAD_DIM] for g in range(Q_PER_KV)]
        qh = jnp.concatenate(heads, axis=0).astype(BF16)
        s = lax.dot_general(qh, kh, (((1,), (1,)), ((), ())), preferred_element_type=F32)
        if bias is not None:
            s = s + jnp.concatenate([bias] * Q_PER_KV, axis=0)
        sink = jnp.full((Q_PER_KV * rows, 1), sink_ref[h * Q_PER_KV], F32)
        for g in range(1, Q_PER_KV):
            sink = jnp.where(rid >= g * rows, sink_ref[h * Q_PER_KV + g], sink)
        m = jnp.maximum(jnp.max(s, axis=-1, keepdims=True), sink)
        e = jnp.exp(s - m)
        denom = jnp.sum(e, axis=-1, keepdims=True) + jnp.exp(sink - m)
        o = jnp.dot(e.astype(BF16), vh, preferred_element_type=F32) * (1.0 / denom)
        for g in range(Q_PER_KV):
            c0 = (h * Q_PER_KV + g) * HEAD_DIM
            o_ref[:, c0:c0 + HEAD_DIM] = o[g * rows:(g + 1) * rows, :].astype(o_ref.dtype)


def _attn_latent_kernel(sink_ref, q_ref, kp_ref, kc_ref, kn_ref, vp_ref, vc_ref, vn_ref, kx_ref, vx_ref,
                        cq_ref, cp_ref, cn_ref, o_ref, *, n_blocks):
    n = pl.program_id(1)
    rows = q_ref.shape[0]
    scale = HEAD_DIM ** -0.5

    def tables(ref):
        return ref[0], ref[1], ref[2]

    q = _rope(q_ref[...], *tables(cq_ref)) * scale
    kb = [_rope(kp_ref[...], *tables(cp_ref)), _rope(kc_ref[...], *tables(cq_ref)), _rope(kn_ref[...], *tables(cn_ref))]
    keys = jnp.concatenate([k.astype(BF16) for k in kb] + [kx_ref[...].astype(BF16)], axis=0)
    vals = jnp.concatenate([vp_ref[...].astype(BF16), vc_ref[...].astype(BF16), vn_ref[...].astype(BF16),
                            vx_ref[...].astype(BF16)], axis=0)
    nk = keys.shape[0]
    qi = lax.broadcasted_iota(jnp.int32, (rows, nk), 0)
    kj = lax.broadcasted_iota(jnp.int32, (rows, nk), 1)
    rel = kj - qi
    lo = jnp.where(n > 0, 0, rows)
    hi = jnp.where(n < n_blocks - 1, 3 * rows, 2 * rows)
    ok = (rel >= 0) & (rel <= 2 * WINDOW) & (kj >= lo) & (kj < hi)
    ok = ok | (kj >= 3 * rows)
    bias = jnp.where(ok, 0.0, NEG_INF).astype(F32)
    _attn_core(q, keys, vals, bias, sink_ref, o_ref)


def _attn_ctx_kernel(sink_ref, q_ref, kx_ref, vx_ref, latent_out_ref, o_ref):
    del latent_out_ref
    q = q_ref[...] * (HEAD_DIM ** -0.5)
    _attn_core(q, kx_ref[...].astype(BF16), vx_ref[...].astype(BF16), None, sink_ref, o_ref)


def _rope_tables(s):
    t = jnp.arange(s)
    row = (t // GRID_W).astype(F32)[:, None]
    col = (t % GRID_W).astype(F32)[:, None]
    half = HEAD_DIM // 2
    inv = ROPE_BASE ** (-jnp.arange(0, half, 2, dtype=F32) / half)
    lane = jnp.arange(2 * HEAD_DIM)
    d = lane % HEAD_DIM
    freq = inv[(d % half) % (half // 2)][None, :]
    ang = jnp.where((d < half)[None, :], row * freq, col * freq)
    first = ((d % half) < half // 2)[None, :]
    sin = jnp.sin(ang)
    return jnp.stack([jnp.cos(ang), jnp.where(first, -sin, 0.0), jnp.where(first, 0.0, sin)]).astype(F32)


def _attention(z, sink, rope_tab, lay):
    n_tok = z.shape[0]
    b, s, n_ctx, nc = lay["b"], lay["s"], lay["n_ctx"], lay["nc"]
    nb = s // CHUNK
    cb0 = nc // CHUNK
    kvw = N_KV_HEADS * HEAD_DIM
    qcol, kcol, vcol = COL_Q // D_BRANCH, COL_K // kvw, COL_V // kvw
    lat = lambda bb, n: cb0 + bb * nb + n
    prev = lambda n: jnp.maximum(n - 1, 0)
    nxt = lambda n: jnp.minimum(n + 1, nb - 1)
    smem = pl.BlockSpec(memory_space=pltpu.SMEM)
    out_lat = pl.pallas_call(
        functools.partial(_attn_latent_kernel, n_blocks=nb),
        out_shape=jax.ShapeDtypeStruct((n_tok, D_BRANCH), BF16),
        grid=(b, nb),
        in_specs=[
            smem,
            pl.BlockSpec((CHUNK, D_BRANCH), lambda bb, n: (lat(bb, n), qcol)),
            pl.BlockSpec((CHUNK, kvw), lambda bb, n: (lat(bb, prev(n)), kcol)),
            pl.BlockSpec((CHUNK, kvw), lambda bb, n: (lat(bb, n), kcol)),
            pl.BlockSpec((CHUNK, kvw), lambda bb, n: (lat(bb, nxt(n)), kcol)),
            pl.BlockSpec((CHUNK, kvw), lambda bb, n: (lat(bb, prev(n)), vcol)),
            pl.BlockSpec((CHUNK, kvw), lambda bb, n: (lat(bb, n), vcol)),
            pl.BlockSpec((CHUNK, kvw), lambda bb, n: (lat(bb, nxt(n)), vcol)),
            pl.BlockSpec((n_ctx, kvw), lambda bb, n: (bb, kcol)),
            pl.BlockSpec((n_ctx, kvw), lambda bb, n: (bb, vcol)),
            pl.BlockSpec((3, CHUNK, kvw), lambda bb, n: (0, n, 0)),
            pl.BlockSpec((3, CHUNK, kvw), lambda bb, n: (0, prev(n), 0)),
            pl.BlockSpec((3, CHUNK, kvw), lambda bb, n: (0, nxt(n), 0)),
        ],
        out_specs=pl.BlockSpec((CHUNK, D_BRANCH), lambda bb, n: (lat(bb, n), 0)),
        compiler_params=_cparams("arbitrary", "arbitrary"),
        name="attention_latent",
    )(sink, z, z, z, z, z, z, z, z, z, rope_tab, rope_tab, rope_tab)
    ncb = n_ctx // CHUNK
    return pl.pallas_call(
        _attn_ctx_kernel,
        out_shape=jax.ShapeDtypeStruct((n_tok, D_BRANCH), BF16),
        grid=(b, ncb),
        in_specs=[
            smem,
            pl.BlockSpec((CHUNK, D_BRANCH), lambda bb, n: (bb * ncb + n, qcol)),
            pl.BlockSpec((n_ctx, kvw), lambda bb, n: (bb, kcol)),
            pl.BlockSpec((n_ctx, kvw), lambda bb, n: (bb, vcol)),
            pl.BlockSpec(memory_space=pl.ANY),
        ],
        out_specs=pl.BlockSpec((CHUNK, D_BRANCH), lambda bb, n: (bb * ncb + n, 0)),
        input_output_aliases={4: 0},
        compiler_params=_cparams("arbitrary", "arbitrary"),
        name="attention_ctx",
    )(sink, z, z, z, out_lat)


def _s5_tables(a_re, a_im, log_step, b_re, b_im, c_re, c_im):
    t_len = CHUNK
    hi = lax.Precision.HIGHEST
    j = jnp.arange(t_len + 1, dtype=F32)
    lam = lax.complex(a_re.astype(F32), a_im.astype(F32))
    log_lbar = lam * jnp.exp(log_step.astype(F32))[..., None]
    lbar = jnp.exp(log_lbar)
    b = lax.complex(b_re.astype(F32), b_im.astype(F32))
    bbar = ((lbar - 1.0) / lam)[..., None] * b[None]
    cc = lax.complex(c_re.astype(F32), c_im.astype(F32))
    pw = jnp.exp(log_lbar[:, :, None, :] * j[None, None, :, None])
    pw_re, pw_im = pw.real, pw.imag

    q = cc[:, :, :, None, :] * bbar.transpose(0, 1, 3, 2)[:, :, None, :, :]
    taps = (jnp.einsum("dgcip,dgjp->dgjci", q.real, pw_re[:, :, :t_len], precision=hi)
            - jnp.einsum("dgcip,dgjp->dgjci", q.imag, pw_im[:, :, :t_len], precision=hi))
    kf, kb = taps[0], taps[1]
    kc = jnp.concatenate([kb[:, :0:-1], (kf[:, :1] + kb[:, :1]), kf[:, 1:]], axis=1)
    strip = kc.transpose(0, 3, 1, 2).reshape(S5_GROUPS, S5_GW, (2 * t_len - 1) * S5_GW)
    width = 2 * t_len * S5_GW
    strip = jnp.pad(strip, ((0, 0), (0, 0), (0, width + 128 - strip.shape[-1])))

    def lanes4(f, bk):
        return jnp.concatenate([f, bk, f, bk], axis=-1)

    bt = bbar.transpose(0, 1, 3, 2)
    p1 = lanes4(pw_re[0, :, t_len - 1::-1], pw_re[1, :, :t_len])
    p2 = lanes4(pw_im[0, :, t_len - 1::-1], pw_im[1, :, :t_len])
    x1 = jnp.concatenate([bt[0].real, bt[1].real, bt[0].imag, bt[1].imag], axis=-1)
    x2 = jnp.concatenate([-bt[0].imag, -bt[1].imag, bt[0].real, bt[1].real], axis=-1)
    q1 = lanes4(pw_re[0, :, 1:t_len + 1], pw_re[1, :, t_len:0:-1])
    q2 = lanes4(pw_im[0, :, 1:t_len + 1], pw_im[1, :, t_len:0:-1])
    y1 = jnp.concatenate([cc[0].real, cc[1].real, -cc[0].imag, -cc[1].imag], axis=-1)
    y2 = jnp.concatenate([-cc[0].imag, -cc[1].imag, -cc[0].real, -cc[1].real], axis=-1)
    rows = jnp.stack([p1, p2, q1, q2], axis=1)
    cols = jnp.stack([x1, x2, y1, y2], axis=1)

    dre = jnp.concatenate([pw_re[0, :, t_len], pw_re[1, :, t_len]], axis=-1)
    dim = jnp.concatenate([pw_im[0, :, t_len], pw_im[1, :, t_len]], axis=-1)
    decay = jnp.stack([dre, dim], axis=1)
    return strip.astype(F32), rows.astype(F32), cols.astype(F32), decay.astype(F32)


def _s5_kernel(u_ref, strip_ref, rows_ref, cols_ref, dec_ref, dskip_ref, y_ref,
               mt_ref, strips_ref, wst_ref, clt_ref, dsre_ref, dsim_ref, fre_ref, fim_ref, bre_ref, bim_ref,
               *, batch, n_ctx_steps, n_lat_steps):
    t_len = CHUNK
    width = t_len * S5_GW
    lanes = 2 * S5_STATE
    strip = strip_ref[0]
    for r in range(8):
        strips_ref[r] = strip[:, r * S5_GW:r * S5_GW + 2 * width].astype(BF16)
    for s in range(t_len):
        off = (t_len - 1 - s) * S5_GW
        r, q = (off // S5_GW) % 8, off // 128
        mt_ref[s * S5_GW:(s + 1) * S5_GW, :] = strips_ref[r, :, q * 128:q * 128 + width]
    for dst_ref, k in ((wst_ref, 0), (clt_ref, 2)):
        col_a, col_b = cols_ref[0, k], cols_ref[0, k + 1]
        for s in range(t_len):
            blk = col_a * rows_ref[0, k, s:s + 1, :] + col_b * rows_ref[0, k + 1, s:s + 1, :]
            dst_ref[s * S5_GW:(s + 1) * S5_GW, :] = blk.astype(BF16)
    u = u_ref[0]
    ds = jnp.dot(u, wst_ref[...], preferred_element_type=F32)
    dsre_ref[...] = ds[:, :lanes]
    dsim_ref[...] = ds[:, lanes:]

    dre = jnp.broadcast_to(dec_ref[0, 0:1, :], (batch, lanes))
    dim = jnp.broadcast_to(dec_ref[0, 1:2, :], (batch, lanes))
    is_fwd = lax.broadcasted_iota(jnp.int32, (batch, lanes), 1) < S5_STATE

    def run(first_row, n_steps, carry):
        def step(it, carry):
            sre, sim = carry
            rows_f = pl.ds(first_row + it, batch, stride=n_steps)
            rows_b = pl.ds(first_row + n_steps - 1 - it, batch, stride=n_steps)
            fre_ref[rows_f, :] = sre
            fim_ref[rows_f, :] = sim
            bre_ref[rows_b, :] = sre
            bim_ref[rows_b, :] = sim
            add_re = jnp.where(is_fwd, dsre_ref[rows_f, :], dsre_ref[rows_b, :])
            add_im = jnp.where(is_fwd, dsim_ref[rows_f, :], dsim_ref[rows_b, :])
            return sre * dre - sim * dim + add_re, sre * dim + sim * dre + add_im

        return lax.fori_loop(0, n_steps, step, carry)

    zero = jnp.zeros((batch, lanes), F32)
    carry = run(0, n_ctx_steps, (zero, zero))
    run(batch * n_ctx_steps, n_lat_steps, carry)

    n_rows = dsre_ref.shape[0]
    fwd_rows = lax.broadcasted_iota(jnp.int32, (n_rows, lanes), 1) < S5_STATE
    sin = jnp.concatenate([jnp.where(fwd_rows, fre_ref[...], bre_ref[...]),
                           jnp.where(fwd_rows, fim_ref[...], bim_ref[...])], axis=1).astype(BF16)
    y = jnp.dot(u, mt_ref[...], preferred_element_type=F32)
    y = y + lax.dot_general(sin, clt_ref[...], (((1,), (1,)), ((), ())), preferred_element_type=F32)
    y_ref[0] = _gelu_tanh(y + dskip_ref[0] * u.astype(F32)).astype(y_ref.dtype)


def _s5_mix(ud, tables, d_skip, lay):
    b, s, n_ctx = lay["b"], lay["s"], lay["n_ctx"]
    strip, rows, cols, decay = tables
    width = CHUNK * S5_GW
    ncs, nls = n_ctx // CHUNK, s // CHUNK
    m = b * (ncs + nls)
    ug = ud.reshape(m, CHUNK, S5_GROUPS, S5_GW).transpose(2, 0, 1, 3).reshape(S5_GROUPS, m, width)
    dskip = jnp.broadcast_to(d_skip.astype(F32).reshape(S5_GROUPS, 1, 1, S5_GW),
                             (S5_GROUPS, 1, CHUNK, S5_GW)).reshape(S5_GROUPS, 1, width)
    state_rows = pltpu.VMEM((m, 2 * S5_STATE), F32)
    y = pl.pallas_call(
        functools.partial(_s5_kernel, batch=b, n_ctx_steps=ncs, n_lat_steps=nls),
        out_shape=jax.ShapeDtypeStruct((S5_GROUPS, m, width), BF16),
        grid=(S5_GROUPS,),
        in_specs=[
            pl.BlockSpec((1, m, width), lambda g: (g, 0, 0)),
            pl.BlockSpec((1, S5_GW, 2 * width + 128), lambda g: (g, 0, 0)),
            pl.BlockSpec((1, 4, CHUNK, 4 * S5_STATE), lambda g: (g, 0, 0, 0)),
            pl.BlockSpec((1, 4, S5_GW, 4 * S5_STATE), lambda g: (g, 0, 0, 0)),
            pl.BlockSpec((1, 2, 2 * S5_STATE), lambda g: (g, 0, 0)),
            pl.BlockSpec((1, 1, width), lambda g: (g, 0, 0)),
        ],
        out_specs=pl.BlockSpec((1, m, width), lambda g: (g, 0, 0)),
        scratch_shapes=[
            pltpu.VMEM((width, width), BF16),
            pltpu.VMEM((8, S5_GW, 2 * width), BF16),
            pltpu.VMEM((width, 4 * S5_STATE), BF16),
            pltpu.VMEM((width, 4 * S5_STATE), BF16),
            state_rows, state_rows, state_rows, state_rows, state_rows, state_rows,
        ],
        compiler_params=_cparams("arbitrary"),
        name="s5_scan",
    )(ug, strip, rows, cols, decay, dskip)
    return y.reshape(S5_GROUPS, m, CHUNK, S5_GW).transpose(1, 2, 0, 3).reshape(m * CHUNK, D_BRANCH)


def _s5_readout_kernel(y_ref, w_ref, o_ref):
    r = jnp.dot(y_ref[...], w_ref[...], preferred_element_type=F32)
    o_ref[...] = (r[:, :D_BRANCH] * _sigmoid(r[:, D_BRANCH:])).astype(o_ref.dtype)


def _s5_readout(y, w_glu, layer, lay):
    n_tok = y.shape[0]
    rows = lay["tm"]
    return pl.pallas_call(
        _s5_readout_kernel,
        out_shape=jax.ShapeDtypeStruct((n_tok, D_BRANCH), BF16),
        grid=(n_tok // rows,),
        in_specs=[
            pl.BlockSpec((rows, D_BRANCH), lambda i: (i, 0)),
            pl.BlockSpec((None, D_BRANCH, 2 * D_BRANCH), lambda i: (layer, 0, 0)),
        ],
        out_specs=pl.BlockSpec((rows, D_BRANCH), lambda i: (i, 0)),
        compiler_params=_cparams("arbitrary"),
        name="s5_readout",
    )(y, w_glu)


def _merge_kernel(h_ref, ba_ref, bb_ref, bc_ref, bd_ref, wg_ref, bg_ref, wb_ref, o_ref):
    h = h_ref[...]
    acc = None
    for k, br_ref in enumerate((ba_ref, bb_ref, bc_ref, bd_ref)):
        gate = _sigmoid(jnp.dot(h, wg_ref[k], preferred_element_type=F32) + bg_ref[k])
        term = gate * jnp.dot(br_ref[...], wb_ref[k], preferred_element_type=F32)
        acc = term if acc is None else acc + term
    o_ref[...] = acc.astype(o_ref.dtype)


def _merge(h, branches, w_gate, b_gate, w_branch, layer, first, lay):
    n_tok, d = h.shape
    tm, tn = lay["tm"], 512
    br_spec = pl.BlockSpec((tm, D_BRANCH), lambda i, j: (i + first, 0))
    return pl.pallas_call(
        _merge_kernel,
        out_shape=jax.ShapeDtypeStruct((n_tok, d), BF16),
        grid=(n_tok // tm - first, d // tn),
        in_specs=[
            pl.BlockSpec((tm, d), lambda i, j: (i + first, 0)),
            br_spec, br_spec, br_spec, br_spec,
            pl.BlockSpec((None, N_BRANCH, d, tn), lambda i, j: (layer, 0, 0, j)),
            pl.BlockSpec((N_BRANCH, 1, tn), lambda i, j: (0, 0, j)),
            pl.BlockSpec((None, N_BRANCH, D_BRANCH, tn), lambda i, j: (layer, 0, 0, j)),
        ],
        out_specs=pl.BlockSpec((tm, tn), lambda i, j: (i + first, j)),
        compiler_params=_cparams("arbitrary", "arbitrary"),
        name="branch_merge",
    )(h, *branches, w_gate, b_gate.reshape(N_BRANCH, 1, d), w_branch)


def _outproj_kernel(m_ref, w_ref, x_ref, gate_ref, o_ref):
    o_ref[...] = x_ref[...] + gate_ref[0] * jnp.dot(m_ref[...], w_ref[...], preferred_element_type=F32)


def _out_projection(merged, w_out, xall, mod3, layer, first, lay):
    n_tok, d = xall.shape
    tm, tn = lay["tm"], 512
    mod_row = lay["mod_row"]
    return pl.pallas_call(
        _outproj_kernel,
        out_shape=jax.ShapeDtypeStruct((n_tok, d), F32),
        grid=(n_tok // tm - first, d // tn),
        in_specs=[
            pl.BlockSpec((tm, d), lambda i, j: (i + first, 0)),
            pl.BlockSpec((None, d, tn), lambda i, j: (layer, 0, j)),
            pl.BlockSpec((tm, tn), lambda i, j: (i + first, j)),
            pl.BlockSpec((1, 1, tn), lambda i, j: (mod_row(i + first, 2), 0, j)),
        ],
        out_specs=pl.BlockSpec((tm, tn), lambda i, j: (i + first, j)),
        compiler_params=_cparams("arbitrary", "arbitrary"),
        name="out_projection",
    )(merged, w_out, xall, mod3)


def _ffn_kernel(x_ref, g_ref, shift_ref, scale_ref, gate_ref, w1_ref, w2_ref, o_ref, h_ref):
    j = pl.program_id(1)

    @pl.when(j == 0)
    def _():
        _modulated_rms_norm(x_ref, g_ref, shift_ref, scale_ref, h_ref)
        o_ref[...] = jnp.zeros_like(o_ref)

    a = jnp.maximum(jnp.dot(h_ref[...], w1_ref[...], preferred_element_type=F32), 0.0)
    o_ref[...] += jnp.dot((a * a).astype(BF16), w2_ref[...], preferred_element_type=F32)

    @pl.when(j == pl.num_programs(1) - 1)
    def _():
        o_ref[...] = x_ref[...] + gate_ref[0] * o_ref[...]


def _ffn(xall, norm_g, mod3, w1, w2, layer, first, lay):
    n_tok, d = xall.shape
    tm, tf = lay["tm"], 512
    mod_row = lay["mod_row"]
    mspec = lambda which: pl.BlockSpec((1, 1, d), lambda i, j: (mod_row(i + first, which), 0, 0))
    return pl.pallas_call(
        _ffn_kernel,
        out_shape=jax.ShapeDtypeStruct((n_tok, d), F32),
        grid=(n_tok // tm - first, D_FF // tf),
        in_specs=[
            pl.BlockSpec((tm, d), lambda i, j: (i + first, 0)),
            pl.BlockSpec((1, d), lambda i, j: (0, 0)),
            mspec(3), mspec(4), mspec(5),
            pl.BlockSpec((None, d, tf), lambda i, j: (layer, 0, j)),
            pl.BlockSpec((None, tf, d), lambda i, j: (layer, j, 0)),
        ],
        out_specs=pl.BlockSpec((tm, d), lambda i, j: (i + first, 0)),
        scratch_shapes=[pltpu.VMEM((tm, d), BF16)],
        compiler_params=_cparams("arbitrary", "arbitrary"),
        name="ffn",
    )(xall, norm_g.reshape(1, d), mod3, mod3, mod3, w1, w2)


def _final_norm_kernel(x_ref, g_ref, o_ref):
    x = x_ref[...]
    o_ref[...] = x * lax.rsqrt(jnp.mean(x * x, axis=-1, keepdims=True) + EPS) * g_ref[...]


def _final_norm(xall, g, lay):
    d = xall.shape[1]
    tm = lay["tm"]
    first = lay["nc"] // tm
    n_lat = lay["b"] * lay["s"]
    return pl.pallas_call(
        _final_norm_kernel,
        out_shape=jax.ShapeDtypeStruct((n_lat, d), F32),
        grid=(n_lat // tm,),
        in_specs=[pl.BlockSpec((tm, d), lambda i: (i + first, 0)), pl.BlockSpec((1, d), lambda i: (0, 0))],
        out_specs=pl.BlockSpec((tm, d), lambda i: (i, 0)),
        compiler_params=_cparams("arbitrary"),
        name="final_norm",
    )(xall, g.reshape(1, d))


def _reorder_in_columns(w_in):
    q_end = 5 * D_BRANCH
    kv = 2 * N_KV_HEADS * HEAD_DIM
    pad = jnp.zeros(w_in.shape[:-1] + (IN_COLS_PAD - IN_COLS,), w_in.dtype)
    return jnp.concatenate([w_in[..., :q_end], w_in[..., q_end + kv:], w_in[..., q_end:q_end + kv], pad], axis=-1)


def kernel(x, c, ctx, c_ctx, w_mod, b_mod, norm1_g, norm2_g, w_in, gmlp_ln_g, gmlp_ln_b, gmlp_ws, gmlp_bs, conv_w, conv_b, conv_ln_g, conv_ln_b, attn_sink, s5_a_re, s5_a_im, s5_log_step, s5_b_re, s5_b_im, s5_c_re, s5_c_im, s5_d, s5_w_glu, w_branch, w_gate, b_gate, w_out, w_ff1, w_ff2, final_g):
    b, s, d = x.shape
    n_ctx = ctx.shape[1]
    depth = w_mod.shape[0]
    assert d == D_MODEL and s % GRID_W == 0
    nc = b * n_ctx
    tm = _row_tile(nc, s)
    tseq = min(256, tm)
    nct, lpt = nc // tm, s // tm

    def mod_row(i, which):
        bidx = jnp.where(i < nct, b, (i - nct) // lpt)
        return bidx * N_MOD + which

    lay = dict(b=b, s=s, n_ctx=n_ctx, nc=nc, tm=tm, tseq=tseq, mod_row=mod_row)

    cond = jnp.concatenate([c, c_ctx[None, :]], axis=0)
    cond = jnp.pad(cond, ((0, -cond.shape[0] % 8), (0, 0)))
    mod_all = _modulation(cond, w_mod, b_mod)
    rope_tab = _rope_tables(s)
    xall = jnp.concatenate([ctx.reshape(nc, d), x.reshape(b * s, d)], axis=0)

    w_in_b = _reorder_in_columns(w_in).astype(BF16)
    w_gate_b, w_branch_b, w_out_b = w_gate.astype(BF16), w_branch.astype(BF16), w_out.astype(BF16)
    w_ff1_b, w_ff2_b, w_glu_b = w_ff1.astype(BF16), w_ff2.astype(BF16), s5_w_glu.astype(BF16)

    for l in range(depth):
        first = nct if l == depth - 1 else 0
        mod3 = mod_all[l, :b + 1].reshape((b + 1) * N_MOD, 1, d)
        h, z, ud = _in_projection(xall, norm1_g[l], mod3, w_in_b, l, lay)
        br_a = _gmlp(z, gmlp_ln_g[l], gmlp_ln_b[l], gmlp_ws[l], gmlp_bs[l], lay)
        br_b = _conformer_conv(z, conv_w[l], conv_b[l], conv_ln_g[l], conv_ln_b[l], lay)
        br_c = _attention(z, attn_sink[l], rope_tab, lay)
        tables = _s5_tables(s5_a_re[l], s5_a_im[l], s5_log_step[l], s5_b_re[l], s5_b_im[l], s5_c_re[l], s5_c_im[l])
        y = _s5_mix(ud, tables, s5_d[l], lay)
        br_d = _s5_readout(y, w_glu_b, l, lay)
        merged = _merge(h, (br_a, br_b, br_c, br_d), w_gate_b, b_gate[l], w_branch_b, l, first, lay)
        xall = _out_projection(merged, w_out_b, xall, mod3, l, first, lay)
        xall = _ffn(xall, norm2_g[l], mod3, w_ff1_b, w_ff2_b, l, first, lay)

    return _final_norm(xall, final_g, lay).reshape(b, s, d)
```

```python
import functools
import math

import jax
import jax.numpy as jnp
from jax import lax
from jax.experimental import pallas as pl
from jax.experimental.pallas import tpu as pltpu

D_MODEL = 2048
N_BRANCH = 4
D_BRANCH = D_MODEL // N_BRANCH
CHUNK = 128
GMLP_GROUPS = 4
CONV_W = 31
CONV_HALO = 16
HEAD_DIM = 64
N_Q_HEADS = D_BRANCH // HEAD_DIM
N_KV_HEADS = 2
Q_PER_KV = N_Q_HEADS // N_KV_HEADS
WINDOW = 128
ROPE_BASE = 10000.0
GRID_W = 64
S5_GW = 16
S5_GROUPS = D_BRANCH // S5_GW
S5_STATE = 64
D_FF = 4 * D_MODEL
N_MOD = 6
EPS = 1e-6
NEG_INF = -1e30

COL_A = 0
COL_B = 2 * D_BRANCH
COL_Q = 4 * D_BRANCH
COL_D = 5 * D_BRANCH
COL_K = 6 * D_BRANCH
COL_V = COL_K + N_KV_HEADS * HEAD_DIM
IN_COLS = COL_V + N_KV_HEADS * HEAD_DIM
IN_COLS_PAD = -(-IN_COLS // D_BRANCH) * D_BRANCH

VMEM_LIMIT_V7X = 56 * 1024 * 1024

F32 = jnp.float32
BF16 = jnp.bfloat16


def _cparams(*sem):
    return pltpu.CompilerParams(dimension_semantics=sem, vmem_limit_bytes=VMEM_LIMIT_V7X)


def _gelu_tanh(x):
    return 0.5 * x * (1.0 + jnp.tanh(math.sqrt(2.0 / math.pi) * (x + 0.044715 * (x * x * x))))


def _sigmoid(x):
    return 1.0 / (1.0 + jnp.exp(-x))


def _layer_norm_rows(v, g, b):
    mu = jnp.mean(v, axis=-1, keepdims=True)
    vc = v - mu
    var = jnp.mean(vc * vc, axis=-1, keepdims=True)
    return vc * lax.rsqrt(var + EPS) * g + b


def _row_tile(nc, s):
    for t in (1024, 512, 256, 128):
        if nc % t == 0 and s % t == 0:
            return t
    raise ValueError("context and sequence lengths must be multiples of 128")


def _mod_kernel(c_ref, w_ref, b_ref, o_ref):
    c = c_ref[...]
    c = (c * _sigmoid(c)).astype(BF16)
    o_ref[0] = jnp.dot(c, w_ref[0].astype(BF16), preferred_element_type=F32) + b_ref[0]


def _modulation(cond, w_mod, b_mod):
    depth, d, n = w_mod.shape
    rows = cond.shape[0]
    tn = 1024
    return pl.pallas_call(
        _mod_kernel,
        out_shape=jax.ShapeDtypeStruct((depth, rows, n), F32),
        grid=(depth, n // tn),
        in_specs=[
            pl.BlockSpec((rows, d), lambda l, j: (0, 0)),
            pl.BlockSpec((1, d, tn), lambda l, j: (l, 0, j)),
            pl.BlockSpec((1, 1, tn), lambda l, j: (l, 0, j)),
        ],
        out_specs=pl.BlockSpec((1, rows, tn), lambda l, j: (l, 0, j)),
        compiler_params=_cparams("arbitrary", "arbitrary"),
        name="modulation",
    )(cond, w_mod, b_mod.reshape(depth, 1, n))


NORM_ROWS = 128


def _modulated_rms_norm(x_ref, g_ref, shift_ref, scale_ref, h_ref):
    gain = g_ref[...] * (1.0 + scale_ref[0])
    shift = shift_ref[0]

    def body(r, carry):
        rows = pl.ds(pl.multiple_of(r * NORM_ROWS, NORM_ROWS), NORM_ROWS)
        x = x_ref[rows, :]
        inv = lax.rsqrt(jnp.mean(x * x, axis=-1, keepdims=True) + EPS)
        h_ref[rows, :] = (x * inv * gain + shift).astype(h_ref.dtype)
        return carry

    lax.fori_loop(0, x_ref.shape[0] // NORM_ROWS, body, 0)


def _inproj_kernel(x_ref, g_ref, shift_ref, scale_ref, w_ref, h_ref, z_ref):
    @pl.when(pl.program_id(1) == 0)
    def _():
        _modulated_rms_norm(x_ref, g_ref, shift_ref, scale_ref, h_ref)

    z_ref[...] = jnp.dot(h_ref[...], w_ref[...], preferred_element_type=F32).astype(z_ref.dtype)


def _in_projection(xall, norm_g, mod3, w_in, layer, lay):
    n_tok, d = xall.shape
    tm, tn = lay["tm"], D_BRANCH
    mod_row = lay["mod_row"]
    return pl.pallas_call(
        _inproj_kernel,
        out_shape=(jax.ShapeDtypeStruct((n_tok, d), BF16), jax.ShapeDtypeStruct((n_tok, IN_COLS_PAD), BF16)),
        grid=(n_tok // tm, IN_COLS_PAD // tn),
        in_specs=[
            pl.BlockSpec((tm, d), lambda i, j: (i, 0)),
            pl.BlockSpec((1, d), lambda i, j: (0, 0)),
            pl.BlockSpec((1, 1, d), lambda i, j: (mod_row(i, 0), 0, 0)),
            pl.BlockSpec((1, 1, d), lambda i, j: (mod_row(i, 1), 0, 0)),
            pl.BlockSpec((None, d, tn), lambda i, j: (layer, 0, j)),
        ],
        out_specs=(pl.BlockSpec((tm, d), lambda i, j: (i, 0)), pl.BlockSpec((tm, tn), lambda i, j: (i, j))),
        compiler_params=_cparams("arbitrary", "arbitrary"),
        name="in_projection",
    )(xall, norm_g.reshape(1, d), mod3, mod3, w_in)


def _gmlp_kernel(za_ref, lng_ref, lnb_ref, ws_ref, bs_ref, o_ref):
    rows = za_ref.shape[0]
    gw = D_BRANCH // GMLP_GROUPS
    a = _gelu_tanh(za_ref[...].astype(F32))
    u = a[:, :D_BRANCH]
    v = _layer_norm_rows(a[:, D_BRANCH:], lng_ref[...], lnb_ref[...]).astype(BF16)
    for c in range(rows // CHUNK):
        r0 = c * CHUNK
        for g in range(GMLP_GROUPS):
            c0 = g * gw
            mixed = jnp.dot(ws_ref[g], v[r0:r0 + CHUNK, c0:c0 + gw], preferred_element_type=F32) + bs_ref[g]
            o_ref[r0:r0 + CHUNK, c0:c0 + gw] = (u[r0:r0 + CHUNK, c0:c0 + gw] * mixed).astype(o_ref.dtype)


def _gmlp(z, ln_g, ln_b, ws, bs, lay):
    n_tok = z.shape[0]
    rows = lay["tseq"]
    gw = D_BRANCH // GMLP_GROUPS
    bs_full = jnp.broadcast_to(bs[:, :, None], (GMLP_GROUPS, CHUNK, gw)).astype(F32)
    return pl.pallas_call(
        _gmlp_kernel,
        out_shape=jax.ShapeDtypeStruct((n_tok, D_BRANCH), BF16),
        grid=(n_tok // rows,),
        in_specs=[
            pl.BlockSpec((rows, 2 * D_BRANCH), lambda i: (i, COL_A // (2 * D_BRANCH))),
            pl.BlockSpec((1, D_BRANCH), lambda i: (0, 0)),
            pl.BlockSpec((1, D_BRANCH), lambda i: (0, 0)),
            pl.BlockSpec((GMLP_GROUPS, CHUNK, CHUNK), lambda i: (0, 0, 0)),
            pl.BlockSpec((GMLP_GROUPS, CHUNK, gw), lambda i: (0, 0, 0)),
        ],
        out_specs=pl.BlockSpec((rows, D_BRANCH), lambda i: (i, 0)),
        compiler_params=_cparams("arbitrary"),
        name="gmlp_mix",
    )(z, ln_g.reshape(1, -1), ln_b.reshape(1, -1), ws.astype(BF16), bs_full)


def _conv_kernel(zm_ref, zp_ref, zn_ref, w_ref, cb_ref, lng_ref, lnb_ref, o_ref, ybuf, *, seg_tiles):
    rows = zm_ref.shape[0]
    i = pl.program_id(0)
    nct, cpt, lpt = seg_tiles
    pos = jnp.where(i < nct, i % cpt, (i - nct) % lpt)
    seg = jnp.where(i < nct, cpt, lpt)
    keep_prev = (pos != 0).astype(F32)
    keep_next = (pos != seg - 1).astype(F32)

    def glu(ref):
        z = ref[...].astype(F32)
        return z[:, :D_BRANCH] * _sigmoid(z[:, D_BRANCH:])

    ybuf[0:CONV_HALO, :] = glu(zp_ref) * keep_prev
    ybuf[CONV_HALO:CONV_HALO + rows, :] = glu(zm_ref)
    ybuf[CONV_HALO + rows:, :] = glu(zn_ref) * keep_next

    sub, tile = 32, 8
    off = CONV_HALO - CONV_W // 2
    for r in range(rows // sub):
        acc = jnp.zeros((sub, D_BRANCH), F32)
        for res in range(tile):
            part = None
            for k in range(CONV_W):
                if (off + k) % tile != res:
                    continue
                start = r * sub + off + k - res
                term = ybuf[start:start + sub + tile, :] * w_ref[k:k + 1, :]
                part = term if part is None else part + term
            if part is not None:
                acc = acc + part[res:res + sub, :]
        y = _layer_norm_rows(acc + cb_ref[...], lng_ref[...], lnb_ref[...])
        o_ref[r * sub:(r + 1) * sub, :] = (y * _sigmoid(y)).astype(o_ref.dtype)


def _conformer_conv(z, w_dw, b_dw, ln_g, ln_b, lay):
    n_tok = z.shape[0]
    rows = lay["tseq"]
    hb = rows // CONV_HALO
    n_halo_blocks = n_tok // CONV_HALO
    colb = COL_B // (2 * D_BRANCH)
    seg_tiles = (lay["nc"] // rows, lay["n_ctx"] // rows, lay["s"] // rows)
    vec = lambda a: a.reshape(1, -1)
    return pl.pallas_call(
        functools.partial(_conv_kernel, seg_tiles=seg_tiles),
        out_shape=jax.ShapeDtypeStruct((n_tok, D_BRANCH), BF16),
        grid=(n_tok // rows,),
        in_specs=[
            pl.BlockSpec((rows, 2 * D_BRANCH), lambda i: (i, colb)),
            pl.BlockSpec((CONV_HALO, 2 * D_BRANCH), lambda i: (jnp.maximum(i * hb - 1, 0), colb)),
            pl.BlockSpec((CONV_HALO, 2 * D_BRANCH), lambda i: (jnp.minimum((i + 1) * hb, n_halo_blocks - 1), colb)),
            pl.BlockSpec((CONV_W, D_BRANCH), lambda i: (0, 0)),
            pl.BlockSpec((1, D_BRANCH), lambda i: (0, 0)),
            pl.BlockSpec((1, D_BRANCH), lambda i: (0, 0)),
            pl.BlockSpec((1, D_BRANCH), lambda i: (0, 0)),
        ],
        out_specs=pl.BlockSpec((rows, D_BRANCH), lambda i: (i, 0)),
        scratch_shapes=[pltpu.VMEM((rows + 2 * CONV_HALO, D_BRANCH), F32)],
        compiler_params=_cparams("arbitrary"),
        name="conformer_conv",
    )(z, z, z, w_dw, vec(b_dw), vec(ln_g), vec(ln_b))


def _rope(x, cos, sin_a, sin_b):
    reps = x.shape[1] // cos.shape[1]
    if reps > 1:
        cos, sin_a, sin_b = (jnp.concatenate([t] * reps, axis=1) for t in (cos, sin_a, sin_b))
    n = x.shape[1]
    half = HEAD_DIM // 4
    return x * cos + pltpu.roll(x, n - half, 1) * sin_a + pltpu.roll(x, half, 1) * sin_b


def _attn_core(q, keys, vals, bias, sink_ref, o_ref):
    rows = q.shape[0]
    rid = lax.broadcasted_iota(jnp.int32, (Q_PER_KV * rows, 1), 0)
    for h in range(N_KV_HEADS):
        kh = keys[:, h * HEAD_DIM:(h + 1) * HEAD_DIM]
        vh = vals[:, h * HEAD_DIM:(h + 1) * HEAD_DIM]
        heads = [q[:, (h * Q_PER_KV + g) * HEAD_DIM:(h * Q_PER_KV + g + 1) * HEAD_DIM] for g in range(Q_PER_KV)]
        qh = jnp.concatenate(heads, axis=0).astype(BF16)
        s = lax.dot_general(qh, kh, (((1,), (1,)), ((), ())), preferred_element_type=F32)
        if bias is not None:
            s = s + jnp.concatenate([bias] * Q_PER_KV, axis=0)
        sink = jnp.full((Q_PER_KV * rows, 1), sink_ref[h * Q_PER_KV], F32)
        for g in range(1, Q_PER_KV):
            sink = jnp.where(rid >= g * rows, sink_ref[h * Q_PER_KV + g], sink)
        m = jnp.maximum(jnp.max(s, axis=-1, keepdims=True), sink)
        e = jnp.exp(s - m)
        denom = jnp.sum(e, axis=-1, keepdims=True) + jnp.exp(sink - m)
        o = jnp.dot(e.astype(BF16), vh, preferred_element_type=F32) * (1.0 / denom)
        for g in range(Q_PER_KV):
            c0 = (h * Q_PER_KV + g) * HEAD_DIM
            o_ref[:, c0:c0 + HEAD_DIM] = o[g * rows:(g + 1) * rows, :].astype(o_ref.dtype)


def _attn_latent_kernel(sink_ref, q_ref, kp_ref, kc_ref, kn_ref, vp_ref, vc_ref, vn_ref, kx_ref, vx_ref,
                        cq_ref, cp_ref, cn_ref, o_ref, *, n_blocks):
    n = pl.program_id(1)
    rows = q_ref.shape[0]
    scale = HEAD_DIM ** -0.5

    def tables(ref):
        return ref[0], ref[1], ref[2]

    f32 = lambda ref: ref[...].astype(F32)
    q = _rope(f32(q_ref), *tables(cq_ref)) * scale
    kb = [_rope(f32(kp_ref), *tables(cp_ref)), _rope(f32(kc_ref), *tables(cq_ref)), _rope(f32(kn_ref), *tables(cn_ref))]
    keys = jnp.concatenate([k.astype(BF16) for k in kb] + [kx_ref[...].astype(BF16)], axis=0)
    vals = jnp.concatenate([vp_ref[...].astype(BF16), vc_ref[...].astype(BF16), vn_ref[...].astype(BF16),
                            vx_ref[...].astype(BF16)], axis=0)
    nk = keys.shape[0]
    qi = lax.broadcasted_iota(jnp.int32, (rows, nk), 0)
    kj = lax.broadcasted_iota(jnp.int32, (rows, nk), 1)
    rel = kj - qi
    lo = jnp.where(n > 0, 0, rows)
    hi = jnp.where(n < n_blocks - 1, 3 * rows, 2 * rows)
    ok = (rel >= 0) & (rel <= 2 * WINDOW) & (kj >= lo) & (kj < hi)
    ok = ok | (kj >= 3 * rows)
    bias = jnp.where(ok, 0.0, NEG_INF).astype(F32)
    _attn_core(q, keys, vals, bias, sink_ref, o_ref)


def _attn_ctx_kernel(sink_ref, q_ref, kx_ref, vx_ref, latent_out_ref, o_ref):
    del latent_out_ref
    q = q_ref[...].astype(F32) * (HEAD_DIM ** -0.5)
    _attn_core(q, kx_ref[...].astype(BF16), vx_ref[...].astype(BF16), None, sink_ref, o_ref)


def _rope_tables(s):
    t = jnp.arange(s)
    row = (t // GRID_W).astype(F32)[:, None]
    col = (t % GRID_W).astype(F32)[:, None]
    half = HEAD_DIM // 2
    inv = ROPE_BASE ** (-jnp.arange(0, half, 2, dtype=F32) / half)
    lane = jnp.arange(2 * HEAD_DIM)
    d = lane % HEAD_DIM
    freq = inv[(d % half) % (half // 2)][None, :]
    ang = jnp.where((d < half)[None, :], row * freq, col * freq)
    first = ((d % half) < half // 2)[None, :]
    sin = jnp.sin(ang)
    return jnp.stack([jnp.cos(ang), jnp.where(first, -sin, 0.0), jnp.where(first, 0.0, sin)]).astype(F32)


def _attention(z, sink, rope_tab, lay):
    n_tok = z.shape[0]
    b, s, n_ctx, nc = lay["b"], lay["s"], lay["n_ctx"], lay["nc"]
    nb = s // CHUNK
    cb0 = nc // CHUNK
    kvw = N_KV_HEADS * HEAD_DIM
    qcol, kcol, vcol = COL_Q // D_BRANCH, COL_K // kvw, COL_V // kvw
    lat = lambda bb, n: cb0 + bb * nb + n
    prev = lambda n: jnp.maximum(n - 1, 0)
    nxt = lambda n: jnp.minimum(n + 1, nb - 1)
    smem = pl.BlockSpec(memory_space=pltpu.SMEM)
    out_lat = pl.pallas_call(
        functools.partial(_attn_latent_kernel, n_blocks=nb),
        out_shape=jax.ShapeDtypeStruct((n_tok, D_BRANCH), BF16),
        grid=(b, nb),
        in_specs=[
            smem,
            pl.BlockSpec((CHUNK, D_BRANCH), lambda bb, n: (lat(bb, n), qcol)),
            pl.BlockSpec((CHUNK, kvw), lambda bb, n: (lat(bb, prev(n)), kcol)),
            pl.BlockSpec((CHUNK, kvw), lambda bb, n: (lat(bb, n), kcol)),
            pl.BlockSpec((CHUNK, kvw), lambda bb, n: (lat(bb, nxt(n)), kcol)),
            pl.BlockSpec((CHUNK, kvw), lambda bb, n: (lat(bb, prev(n)), vcol)),
            pl.BlockSpec((CHUNK, kvw), lambda bb, n: (lat(bb, n), vcol)),
            pl.BlockSpec((CHUNK, kvw), lambda bb, n: (lat(bb, nxt(n)), vcol)),
            pl.BlockSpec((n_ctx, kvw), lambda bb, n: (bb, kcol)),
            pl.BlockSpec((n_ctx, kvw), lambda bb, n: (bb, vcol)),
            pl.BlockSpec((3, CHUNK, kvw), lambda bb, n: (0, n, 0)),
            pl.BlockSpec((3, CHUNK, kvw), lambda bb, n: (0, prev(n), 0)),
            pl.BlockSpec((3, CHUNK, kvw), lambda bb, n: (0, nxt(n), 0)),
        ],
        out_specs=pl.BlockSpec((CHUNK, D_BRANCH), lambda bb, n: (lat(bb, n), 0)),
        compiler_params=_cparams("arbitrary", "arbitrary"),
        name="attention_latent",
    )(sink, z, z, z, z, z, z, z, z, z, rope_tab, rope_tab, rope_tab)
    ncb = n_ctx // CHUNK
    return pl.pallas_call(
        _attn_ctx_kernel,
        out_shape=jax.ShapeDtypeStruct((n_tok, D_BRANCH), BF16),
        grid=(b, ncb),
        in_specs=[
            smem,
            pl.BlockSpec((CHUNK, D_BRANCH), lambda bb, n: (bb * ncb + n, qcol)),
            pl.BlockSpec((n_ctx, kvw), lambda bb, n: (bb, kcol)),
            pl.BlockSpec((n_ctx, kvw), lambda bb, n: (bb, vcol)),
            pl.BlockSpec(memory_space=pl.ANY),
        ],
        out_specs=pl.BlockSpec((CHUNK, D_BRANCH), lambda bb, n: (bb * ncb + n, 0)),
        input_output_aliases={4: 0},
        compiler_params=_cparams("arbitrary", "arbitrary"),
        name="attention_ctx",
    )(sink, z, z, z, out_lat)


def _s5_tables(a_re, a_im, log_step, b_re, b_im, c_re, c_im):
    t_len = CHUNK
    hi = lax.Precision.HIGHEST
    j = jnp.arange(t_len + 1, dtype=F32)
    lam = lax.complex(a_re.astype(F32), a_im.astype(F32))
    log_lbar = lam * jnp.exp(log_step.astype(F32))[..., None]
    lbar = jnp.exp(log_lbar)
    b = lax.complex(b_re.astype(F32), b_im.astype(F32))
    bbar = ((lbar - 1.0) / lam)[..., None] * b[None]
    cc = lax.complex(c_re.astype(F32), c_im.astype(F32))
    pw = jnp.exp(log_lbar[:, :, None, :] * j[None, None, :, None])
    pw_re, pw_im = pw.real, pw.imag

    q = cc[:, :, :, None, :] * bbar.transpose(0, 1, 3, 2)[:, :, None, :, :]
    taps = (jnp.einsum("dgcip,dgjp->dgjci", q.real, pw_re[:, :, :t_len], precision=hi)
            - jnp.einsum("dgcip,dgjp->dgjci", q.imag, pw_im[:, :, :t_len], precision=hi))
    kf, kb = taps[0], taps[1]
    kc = jnp.concatenate([kb[:, :0:-1], (kf[:, :1] + kb[:, :1]), kf[:, 1:]], axis=1)
    strip = kc.transpose(0, 3, 1, 2).reshape(S5_GROUPS, S5_GW, (2 * t_len - 1) * S5_GW)
    width = 2 * t_len * S5_GW
    strip = jnp.pad(strip, ((0, 0), (0, 0), (0, width + 128 - strip.shape[-1])))

    def lanes4(f, bk):
        return jnp.concatenate([f, bk, f, bk], axis=-1)

    bt = bbar.transpose(0, 1, 3, 2)
    p1 = lanes4(pw_re[0, :, t_len - 1::-1], pw_re[1, :, :t_len])
    p2 = lanes4(pw_im[0, :, t_len - 1::-1], pw_im[1, :, :t_len])
    x1 = jnp.concatenate([bt[0].real, bt[1].real, bt[0].imag, bt[1].imag], axis=-1)
    x2 = jnp.concatenate([-bt[0].imag, -bt[1].imag, bt[0].real, bt[1].real], axis=-1)
    q1 = lanes4(pw_re[0, :, 1:t_len + 1], pw_re[1, :, t_len:0:-1])
    q2 = lanes4(pw_im[0, :, 1:t_len + 1], pw_im[1, :, t_len:0:-1])
    y1 = jnp.concatenate([cc[0].real, cc[1].real, -cc[0].imag, -cc[1].imag], axis=-1)
    y2 = jnp.concatenate([-cc[0].imag, -cc[1].imag, -cc[0].real, -cc[1].real], axis=-1)
    rows = jnp.stack([p1, p2, q1, q2], axis=1)
    cols = jnp.stack([x1, x2, y1, y2], axis=1)

    dre = jnp.concatenate([pw_re[0, :, t_len], pw_re[1, :, t_len]], axis=-1)
    dim = jnp.concatenate([pw_im[0, :, t_len], pw_im[1, :, t_len]], axis=-1)
    decay = jnp.stack([dre, dim], axis=1)
    return strip.astype(F32), rows.astype(F32), cols.astype(F32), decay.astype(F32)


def _s5_kernel(u_ref, strip_ref, rows_ref, cols_ref, dec_ref, dskip_ref, y_ref,
               mt_ref, strips_ref, wst_ref, clt_ref, dsre_ref, dsim_ref, fre_ref, fim_ref, bre_ref, bim_ref,
               *, batch, n_ctx_steps, n_lat_steps):
    t_len = CHUNK
    width = t_len * S5_GW
    lanes = 2 * S5_STATE
    strip = strip_ref[0]
    for r in range(8):
        strips_ref[r] = strip[:, r * S5_GW:r * S5_GW + 2 * width].astype(BF16)
    for s in range(t_len):
        off = (t_len - 1 - s) * S5_GW
        r, q = (off // S5_GW) % 8, off // 128
        mt_ref[s * S5_GW:(s + 1) * S5_GW, :] = strips_ref[r, :, q * 128:q * 128 + width]
    for dst_ref, k in ((wst_ref, 0), (clt_ref, 2)):
        col_a, col_b = cols_ref[0, k], cols_ref[0, k + 1]
        for s in range(t_len):
            blk = col_a * rows_ref[0, k, s:s + 1, :] + col_b * rows_ref[0, k + 1, s:s + 1, :]
            dst_ref[s * S5_GW:(s + 1) * S5_GW, :] = blk.astype(BF16)
    u = u_ref[0]
    ds = jnp.dot(u, wst_ref[...], preferred_element_type=F32)
    dsre_ref[...] = ds[:, :lanes]
    dsim_ref[...] = ds[:, lanes:]

    dre = jnp.broadcast_to(dec_ref[0, 0:1, :], (batch, lanes))
    dim = jnp.broadcast_to(dec_ref[0, 1:2, :], (batch, lanes))
    is_fwd = lax.broadcasted_iota(jnp.int32, (batch, lanes), 1) < S5_STATE

    def run(first_row, n_steps, carry):
        def step(it, carry):
            sre, sim = carry
            rows_f = pl.ds(first_row + it, batch, stride=n_steps)
            rows_b = pl.ds(first_row + n_steps - 1 - it, batch, stride=n_steps)
            fre_ref[rows_f, :] = sre
            fim_ref[rows_f, :] = sim
            bre_ref[rows_b, :] = sre
            bim_ref[rows_b, :] = sim
            add_re = jnp.where(is_fwd, dsre_ref[rows_f, :], dsre_ref[rows_b, :])
            add_im = jnp.where(is_fwd, dsim_ref[rows_f, :], dsim_ref[rows_b, :])
            return sre * dre - sim * dim + add_re, sre * dim + sim * dre + add_im

        return lax.fori_loop(0, n_steps, step, carry)

    zero = jnp.zeros((batch, lanes), F32)
    carry = run(0, n_ctx_steps, (zero, zero))
    run(batch * n_ctx_steps, n_lat_steps, carry)

    n_rows = dsre_ref.shape[0]
    fwd_rows = lax.broadcasted_iota(jnp.int32, (n_rows, lanes), 1) < S5_STATE
    sin = jnp.concatenate([jnp.where(fwd_rows, fre_ref[...], bre_ref[...]),
                           jnp.where(fwd_rows, fim_ref[...], bim_ref[...])], axis=1).astype(BF16)
    y = jnp.dot(u, mt_ref[...], preferred_element_type=F32)
    y = y + lax.dot_general(sin, clt_ref[...], (((1,), (1,)), ((), ())), preferred_element_type=F32)
    y_ref[0] = _gelu_tanh(y + dskip_ref[0] * u.astype(F32)).astype(y_ref.dtype)


def _s5_mix(ud, tables, d_skip, lay):
    b, s, n_ctx = lay["b"], lay["s"], lay["n_ctx"]
    strip, rows, cols, decay = tables
    width = CHUNK * S5_GW
    ncs, nls = n_ctx // CHUNK, s // CHUNK
    m = b * (ncs + nls)
    ug = ud.reshape(m, CHUNK, S5_GROUPS, S5_GW).transpose(2, 0, 1, 3).reshape(S5_GROUPS, m, width)
    dskip = jnp.broadcast_to(d_skip.astype(F32).reshape(S5_GROUPS, 1, 1, S5_GW),
                             (S5_GROUPS, 1, CHUNK, S5_GW)).reshape(S5_GROUPS, 1, width)
    state_rows = pltpu.VMEM((m, 2 * S5_STATE), F32)
    y = pl.pallas_call(
        functools.partial(_s5_kernel, batch=b, n_ctx_steps=ncs, n_lat_steps=nls),
        out_shape=jax.ShapeDtypeStruct((S5_GROUPS, m, width), BF16),
        grid=(S5_GROUPS,),
        in_specs=[
            pl.BlockSpec((1, m, width), lambda g: (g, 0, 0)),
            pl.BlockSpec((1, S5_GW, 2 * width + 128), lambda g: (g, 0, 0)),
            pl.BlockSpec((1, 4, CHUNK, 4 * S5_STATE), lambda g: (g, 0, 0, 0)),
            pl.BlockSpec((1, 4, S5_GW, 4 * S5_STATE), lambda g: (g, 0, 0, 0)),
            pl.BlockSpec((1, 2, 2 * S5_STATE), lambda g: (g, 0, 0)),
            pl.BlockSpec((1, 1, width), lambda g: (g, 0, 0)),
        ],
        out_specs=pl.BlockSpec((1, m, width), lambda g: (g, 0, 0)),
        scratch_shapes=[
            pltpu.VMEM((width, width), BF16),
            pltpu.VMEM((8, S5_GW, 2 * width), BF16),
            pltpu.VMEM((width, 4 * S5_STATE), BF16),
            pltpu.VMEM((width, 4 * S5_STATE), BF16),
            state_rows, state_rows, state_rows, state_rows, state_rows, state_rows,
        ],
        compiler_params=_cparams("arbitrary"),
        name="s5_scan",
    )(ug, strip, rows, cols, decay, dskip)
    return y.reshape(S5_GROUPS, m, CHUNK, S5_GW).transpose(1, 2, 0, 3).reshape(m * CHUNK, D_BRANCH)


def _s5_readout_kernel(y_ref, w_ref, o_ref):
    r = jnp.dot(y_ref[...], w_ref[...], preferred_element_type=F32)
    o_ref[...] = (r[:, :D_BRANCH] * _sigmoid(r[:, D_BRANCH:])).astype(o_ref.dtype)


def _s5_readout(y, w_glu, layer, lay):
    n_tok = y.shape[0]
    rows = lay["tm"]
    return pl.pallas_call(
        _s5_readout_kernel,
        out_shape=jax.ShapeDtypeStruct((n_tok, D_BRANCH), BF16),
        grid=(n_tok // rows,),
        in_specs=[
            pl.BlockSpec((rows, D_BRANCH), lambda i: (i, 0)),
            pl.BlockSpec((None, D_BRANCH, 2 * D_BRANCH), lambda i: (layer, 0, 0)),
        ],
        out_specs=pl.BlockSpec((rows, D_BRANCH), lambda i: (i, 0)),
        compiler_params=_cparams("arbitrary"),
        name="s5_readout",
    )(y, w_glu)


def _merge_kernel(h_ref, ba_ref, bb_ref, bc_ref, bd_ref, wg_ref, bg_ref, wb_ref, o_ref):
    h = h_ref[...]
    acc = None
    for k, br_ref in enumerate((ba_ref, bb_ref, bc_ref, bd_ref)):
        gate = _sigmoid(jnp.dot(h, wg_ref[k], preferred_element_type=F32) + bg_ref[k])
        term = gate * jnp.dot(br_ref[...], wb_ref[k], preferred_element_type=F32)
        acc = term if acc is None else acc + term
    o_ref[...] = acc.astype(o_ref.dtype)


def _merge(h, branches, w_gate, b_gate, w_branch, layer, first, lay):
    n_tok, d = h.shape
    tm, tn = lay["tm"], 512
    br_spec = pl.BlockSpec((tm, D_BRANCH), lambda i, j: (i + first, 0))
    return pl.pallas_call(
        _merge_kernel,
        out_shape=jax.ShapeDtypeStruct((n_tok, d), BF16),
        grid=(n_tok // tm - first, d // tn),
        in_specs=[
            pl.BlockSpec((tm, d), lambda i, j: (i + first, 0)),
            br_spec, br_spec, br_spec, br_spec,
            pl.BlockSpec((None, N_BRANCH, d, tn), lambda i, j: (layer, 0, 0, j)),
            pl.BlockSpec((N_BRANCH, 1, tn), lambda i, j: (0, 0, j)),
            pl.BlockSpec((None, N_BRANCH, D_BRANCH, tn), lambda i, j: (layer, 0, 0, j)),
        ],
        out_specs=pl.BlockSpec((tm, tn), lambda i, j: (i + first, j)),
        compiler_params=_cparams("arbitrary", "arbitrary"),
        name="branch_merge",
    )(h, *branches, w_gate, b_gate.reshape(N_BRANCH, 1, d), w_branch)


def _outproj_kernel(m_ref, w_ref, x_ref, gate_ref, o_ref):
    o_ref[...] = x_ref[...] + gate_ref[0] * jnp.dot(m_ref[...], w_ref[...], preferred_element_type=F32)


def _out_projection(merged, w_out, xall, mod3, layer, first, lay):
    n_tok, d = xall.shape
    split = 2
    tm = lay["tm"] // split
    mod_row = lay["mod_row"]
    skip = first * split
    return pl.pallas_call(
        _outproj_kernel,
        out_shape=jax.ShapeDtypeStruct((n_tok, d), F32),
        grid=(n_tok // tm - skip,),
        in_specs=[
            pl.BlockSpec((tm, d), lambda i: (i + skip, 0)),
            pl.BlockSpec((None, d, d), lambda i: (layer, 0, 0)),
            pl.BlockSpec((tm, d), lambda i: (i + skip, 0)),
            pl.BlockSpec((1, 1, d), lambda i: (mod_row((i + skip) // split, 2), 0, 0)),
        ],
        out_specs=pl.BlockSpec((tm, d), lambda i: (i + skip, 0)),
        compiler_params=_cparams("arbitrary"),
        name="out_projection",
    )(merged, w_out, xall, mod3)


def _ffn_kernel(x_ref, g_ref, shift_ref, scale_ref, gate_ref, w1_ref, w2_ref, o_ref, h_ref):
    j = pl.program_id(1)

    @pl.when(j == 0)
    def _():
        _modulated_rms_norm(x_ref, g_ref, shift_ref, scale_ref, h_ref)
        o_ref[...] = jnp.zeros_like(o_ref)

    a = jnp.maximum(jnp.dot(h_ref[...], w1_ref[...], preferred_element_type=F32), 0.0)
    o_ref[...] += jnp.dot((a * a).astype(BF16), w2_ref[...], preferred_element_type=F32)

    @pl.when(j == pl.num_programs(1) - 1)
    def _():
        o_ref[...] = x_ref[...] + gate_ref[0] * o_ref[...]


def _ffn(xall, norm_g, mod3, w1, w2, layer, first, lay):
    n_tok, d = xall.shape
    tm, tf = lay["tm"], 512
    mod_row = lay["mod_row"]
    mspec = lambda which: pl.BlockSpec((1, 1, d), lambda i, j: (mod_row(i + first, which), 0, 0))
    return pl.pallas_call(
        _ffn_kernel,
        out_shape=jax.ShapeDtypeStruct((n_tok, d), F32),
        grid=(n_tok // tm - first, D_FF // tf),
        in_specs=[
            pl.BlockSpec((tm, d), lambda i, j: (i + first, 0)),
            pl.BlockSpec((1, d), lambda i, j: (0, 0)),
            mspec(3), mspec(4), mspec(5),
            pl.BlockSpec((None, d, tf), lambda i, j: (layer, 0, j)),
            pl.BlockSpec((None, tf, d), lambda i, j: (layer, j, 0)),
        ],
        out_specs=pl.BlockSpec((tm, d), lambda i, j: (i + first, 0)),
        scratch_shapes=[pltpu.VMEM((tm, d), BF16)],
        compiler_params=_cparams("arbitrary", "arbitrary"),
        name="ffn",
    )(xall, norm_g.reshape(1, d), mod3, mod3, mod3, w1, w2)


def _final_norm_kernel(x_ref, g_ref, o_ref):
    x = x_ref[...]
    o_ref[...] = x * lax.rsqrt(jnp.mean(x * x, axis=-1, keepdims=True) + EPS) * g_ref[...]


def _final_norm(xall, g, lay):
    d = xall.shape[1]
    tm = lay["tm"]
    first = lay["nc"] // tm
    n_lat = lay["b"] * lay["s"]
    return pl.pallas_call(
        _final_norm_kernel,
        out_shape=jax.ShapeDtypeStruct((n_lat, d), F32),
        grid=(n_lat // tm,),
        in_specs=[pl.BlockSpec((tm, d), lambda i: (i + first, 0)), pl.BlockSpec((1, d), lambda i: (0, 0))],
        out_specs=pl.BlockSpec((tm, d), lambda i: (i, 0)),
        compiler_params=_cparams("arbitrary"),
        name="final_norm",
    )(xall, g.reshape(1, d))


def _reorder_in_columns(w_in):
    q_end = 5 * D_BRANCH
    kv = 2 * N_KV_HEADS * HEAD_DIM
    pad = jnp.zeros(w_in.shape[:-1] + (IN_COLS_PAD - IN_COLS,), w_in.dtype)
    return jnp.concatenate([w_in[..., :q_end], w_in[..., q_end + kv:], w_in[..., q_end:q_end + kv], pad], axis=-1)


def kernel(x, c, ctx, c_ctx, w_mod, b_mod, norm1_g, norm2_g, w_in, gmlp_ln_g, gmlp_ln_b, gmlp_ws, gmlp_bs, conv_w, conv_b, conv_ln_g, conv_ln_b, attn_sink, s5_a_re, s5_a_im, s5_log_step, s5_b_re, s5_b_im, s5_c_re, s5_c_im, s5_d, s5_w_glu, w_branch, w_gate, b_gate, w_out, w_ff1, w_ff2, final_g):
    b, s, d = x.shape
    n_ctx = ctx.shape[1]
    depth = w_mod.shape[0]
    assert d == D_MODEL and s % GRID_W == 0
    nc = b * n_ctx
    tm = _row_tile(nc, s)
    tseq = min(256, tm)
    nct, lpt = nc // tm, s // tm

    def mod_row(i, which):
        bidx = jnp.where(i < nct, b, (i - nct) // lpt)
        return bidx * N_MOD + which

    lay = dict(b=b, s=s, n_ctx=n_ctx, nc=nc, tm=tm, tseq=tseq, mod_row=mod_row)

    cond = jnp.concatenate([c, c_ctx[None, :]], axis=0)
    cond = jnp.pad(cond, ((0, -cond.shape[0] % 8), (0, 0)))
    mod_all = _modulation(cond, w_mod, b_mod)
    rope_tab = _rope_tables(s)
    xall = jnp.concatenate([ctx.reshape(nc, d), x.reshape(b * s, d)], axis=0)

    w_in_b = _reorder_in_columns(w_in).astype(BF16)
    w_gate_b, w_branch_b, w_out_b = w_gate.astype(BF16), w_branch.astype(BF16), w_out.astype(BF16)
    w_ff1_b, w_ff2_b, w_glu_b = w_ff1.astype(BF16), w_ff2.astype(BF16), s5_w_glu.astype(BF16)

    for l in range(depth):
        first = nct if l == depth - 1 else 0
        mod3 = mod_all[l, :b + 1].reshape((b + 1) * N_MOD, 1, d)
        h, z = _in_projection(xall, norm1_g[l], mod3, w_in_b, l, lay)
        br_a = _gmlp(z, gmlp_ln_g[l], gmlp_ln_b[l], gmlp_ws[l], gmlp_bs[l], lay)
        br_b = _conformer_conv(z, conv_w[l], conv_b[l], conv_ln_g[l], conv_ln_b[l], lay)
        br_c = _attention(z, attn_sink[l], rope_tab, lay)
        tables = _s5_tables(s5_a_re[l], s5_a_im[l], s5_log_step[l], s5_b_re[l], s5_b_im[l], s5_c_re[l], s5_c_im[l])
        y = _s5_mix(z[:, COL_D:COL_D + D_BRANCH], tables, s5_d[l], lay)
        br_d = _s5_readout(y, w_glu_b, l, lay)
        merged = _merge(h, (br_a, br_b, br_c, br_d), w_gate_b, b_gate[l], w_branch_b, l, first, lay)
        xall = _out_projection(merged, w_out_b, xall, mod3, l, first, lay)
        xall = _ffn(xall, norm2_g[l], mod3, w_ff1_b, w_ff2_b, l, first, lay)

    return _final_norm(xall, final_g, lay).reshape(b, s, d)
```

```python
import functools
import math

import jax
import jax.numpy as jnp
from jax import lax
from jax.experimental import pallas as pl
from jax.experimental.pallas import tpu as pltpu

D_MODEL = 2048
N_BRANCH = 4
D_BRANCH = D_MODEL // N_BRANCH
CHUNK = 128
GMLP_GROUPS = 4
CONV_W = 31
CONV_HALO = 16
HEAD_DIM = 64
N_Q_HEADS = D_BRANCH // HEAD_DIM
N_KV_HEADS = 2
Q_PER_KV = N_Q_HEADS // N_KV_HEADS
WINDOW = 128
ROPE_BASE = 10000.0
GRID_W = 64
S5_GW = 16
S5_GROUPS = D_BRANCH // S5_GW
S5_STATE = 64
D_FF = 4 * D_MODEL
N_MOD = 6
EPS = 1e-6
NEG_INF = -1e30

COL_A = 0
COL_B = 2 * D_BRANCH
COL_Q = 4 * D_BRANCH
COL_D = 5 * D_BRANCH
COL_K = 6 * D_BRANCH
COL_V = COL_K + N_KV_HEADS * HEAD_DIM
IN_COLS = COL_V + N_KV_HEADS * HEAD_DIM
IN_COLS_PAD = -(-IN_COLS // D_BRANCH) * D_BRANCH

VMEM_LIMIT_V7X = 56 * 1024 * 1024

F32 = jnp.float32
BF16 = jnp.bfloat16


def _cparams(*sem):
    return pltpu.CompilerParams(dimension_semantics=sem, vmem_limit_bytes=VMEM_LIMIT_V7X)


def _gelu_tanh(x):
    return 0.5 * x * (1.0 + jnp.tanh(math.sqrt(2.0 / math.pi) * (x + 0.044715 * (x * x * x))))


def _sigmoid(x):
    return 1.0 / (1.0 + jnp.exp(-x))


def _layer_norm_rows(v, g, b):
    mu = jnp.mean(v, axis=-1, keepdims=True)
    vc = v - mu
    var = jnp.mean(vc * vc, axis=-1, keepdims=True)
    return vc * lax.rsqrt(var + EPS) * g + b


def _row_tile(nc, s):
    for t in (1024, 512, 256, 128):
        if nc % t == 0 and s % t == 0:
            return t
    raise ValueError("context and sequence lengths must be multiples of 128")


def _mod_kernel(c_ref, w_ref, b_ref, o_ref):
    c = c_ref[...]
    c = (c * _sigmoid(c)).astype(BF16)
    o_ref[0] = jnp.dot(c, w_ref[0].astype(BF16), preferred_element_type=F32) + b_ref[0]


def _modulation(cond, w_mod, b_mod):
    depth, d, n = w_mod.shape
    rows = cond.shape[0]
    tn = 1024
    return pl.pallas_call(
        _mod_kernel,
        out_shape=jax.ShapeDtypeStruct((depth, rows, n), F32),
        grid=(depth, n // tn),
        in_specs=[
            pl.BlockSpec((rows, d), lambda l, j: (0, 0)),
            pl.BlockSpec((1, d, tn), lambda l, j: (l, 0, j)),
            pl.BlockSpec((1, 1, tn), lambda l, j: (l, 0, j)),
        ],
        out_specs=pl.BlockSpec((1, rows, tn), lambda l, j: (l, 0, j)),
        compiler_params=_cparams("arbitrary", "arbitrary"),
        name="modulation",
    )(cond, w_mod, b_mod.reshape(depth, 1, n))


NORM_ROWS = 128


def _modulated_rms_norm(x_ref, g_ref, shift_ref, scale_ref, h_ref):
    gain = g_ref[...] * (1.0 + scale_ref[0])
    shift = shift_ref[0]

    def body(r, carry):
        rows = pl.ds(pl.multiple_of(r * NORM_ROWS, NORM_ROWS), NORM_ROWS)
        x = x_ref[rows, :]
        inv = lax.rsqrt(jnp.mean(x * x, axis=-1, keepdims=True) + EPS)
        h_ref[rows, :] = (x * inv * gain + shift).astype(h_ref.dtype)
        return carry

    lax.fori_loop(0, x_ref.shape[0] // NORM_ROWS, body, 0)


def _inproj_kernel(x_ref, g_ref, shift_ref, scale_ref, w_ref, h_ref, z_ref, ud_ref):
    _modulated_rms_norm(x_ref, g_ref, shift_ref, scale_ref, h_ref)
    h = h_ref[...]
    for c0 in range(0, z_ref.shape[1], D_BRANCH):
        z = jnp.dot(h, w_ref[:, c0:c0 + D_BRANCH], preferred_element_type=F32).astype(z_ref.dtype)
        z_ref[:, c0:c0 + D_BRANCH] = z
        if c0 == COL_D:
            ud_ref[...] = z


def _in_projection(xall, norm_g, mod3, w_in, layer, lay):
    n_tok, d = xall.shape
    split = 2
    tm = lay["tm"] // split
    mod_row = lay["mod_row"]
    return pl.pallas_call(
        _inproj_kernel,
        out_shape=(jax.ShapeDtypeStruct((n_tok, d), BF16), jax.ShapeDtypeStruct((n_tok, IN_COLS_PAD), BF16),
                   jax.ShapeDtypeStruct((n_tok, D_BRANCH), BF16)),
        grid=(n_tok // tm,),
        in_specs=[
            pl.BlockSpec((tm, d), lambda i: (i, 0)),
            pl.BlockSpec((1, d), lambda i: (0, 0)),
            pl.BlockSpec((1, 1, d), lambda i: (mod_row(i // split, 0), 0, 0)),
            pl.BlockSpec((1, 1, d), lambda i: (mod_row(i // split, 1), 0, 0)),
            pl.BlockSpec((None, d, IN_COLS_PAD), lambda i: (layer, 0, 0)),
        ],
        out_specs=(pl.BlockSpec((tm, d), lambda i: (i, 0)), pl.BlockSpec((tm, IN_COLS_PAD), lambda i: (i, 0)),
                   pl.BlockSpec((tm, D_BRANCH), lambda i: (i, 0))),
        compiler_params=_cparams("arbitrary"),
        name="in_projection",
    )(xall, norm_g.reshape(1, d), mod3, mod3, w_in)


def _gmlp_kernel(za_ref, lng_ref, lnb_ref, ws_ref, bs_ref, o_ref):
    rows = za_ref.shape[0]
    gw = D_BRANCH // GMLP_GROUPS
    a = _gelu_tanh(za_ref[...].astype(F32))
    u = a[:, :D_BRANCH]
    v = _layer_norm_rows(a[:, D_BRANCH:], lng_ref[...], lnb_ref[...]).astype(BF16)
    for c in range(rows // CHUNK):
        r0 = c * CHUNK
        for g in range(GMLP_GROUPS):
            c0 = g * gw
            mixed = jnp.dot(ws_ref[g], v[r0:r0 + CHUNK, c0:c0 + gw], preferred_element_type=F32) + bs_ref[g]
            o_ref[r0:r0 + CHUNK, c0:c0 + gw] = (u[r0:r0 + CHUNK, c0:c0 + gw] * mixed).astype(o_ref.dtype)


def _gmlp(z, ln_g, ln_b, ws, bs, lay):
    n_tok = z.shape[0]
    rows = lay["tseq"]
    gw = D_BRANCH // GMLP_GROUPS
    bs_full = jnp.broadcast_to(bs[:, :, None], (GMLP_GROUPS, CHUNK, gw)).astype(F32)
    return pl.pallas_call(
        _gmlp_kernel,
        out_shape=jax.ShapeDtypeStruct((n_tok, D_BRANCH), BF16),
        grid=(n_tok // rows,),
        in_specs=[
            pl.BlockSpec((rows, 2 * D_BRANCH), lambda i: (i, COL_A // (2 * D_BRANCH))),
            pl.BlockSpec((1, D_BRANCH), lambda i: (0, 0)),
            pl.BlockSpec((1, D_BRANCH), lambda i: (0, 0)),
            pl.BlockSpec((GMLP_GROUPS, CHUNK, CHUNK), lambda i: (0, 0, 0)),
            pl.BlockSpec((GMLP_GROUPS, CHUNK, gw), lambda i: (0, 0, 0)),
        ],
        out_specs=pl.BlockSpec((rows, D_BRANCH), lambda i: (i, 0)),
        compiler_params=_cparams("arbitrary"),
        name="gmlp_mix",
    )(z, ln_g.reshape(1, -1), ln_b.reshape(1, -1), ws.astype(BF16), bs_full)


def _conv_kernel(zm_ref, zp_ref, zn_ref, w_ref, cb_ref, lng_ref, lnb_ref, o_ref, ybuf, *, seg_tiles):
    rows = zm_ref.shape[0]
    i = pl.program_id(0)
    nct, cpt, lpt = seg_tiles
    pos = jnp.where(i < nct, i % cpt, (i - nct) % lpt)
    seg = jnp.where(i < nct, cpt, lpt)
    keep_prev = (pos != 0).astype(F32)
    keep_next = (pos != seg - 1).astype(F32)

    def glu(ref):
        z = ref[...].astype(F32)
        return z[:, :D_BRANCH] * _sigmoid(z[:, D_BRANCH:])

    ybuf[0:CONV_HALO, :] = glu(zp_ref) * keep_prev
    ybuf[CONV_HALO:CONV_HALO + rows, :] = glu(zm_ref)
    ybuf[CONV_HALO + rows:, :] = glu(zn_ref) * keep_next

    sub, tile = 32, 8
    off = CONV_HALO - CONV_W // 2
    for r in range(rows // sub):
        acc = jnp.zeros((sub, D_BRANCH), F32)
        for res in range(tile):
            part = None
            for k in range(CONV_W):
                if (off + k) % tile != res:
                    continue
                start = r * sub + off + k - res
                term = ybuf[start:start + sub + tile, :] * w_ref[k:k + 1, :]
                part = term if part is None else part + term
            if part is not None:
                acc = acc + part[res:res + sub, :]
        y = _layer_norm_rows(acc + cb_ref[...], lng_ref[...], lnb_ref[...])
        o_ref[r * sub:(r + 1) * sub, :] = (y * _sigmoid(y)).astype(o_ref.dtype)


def _conformer_conv(z, w_dw, b_dw, ln_g, ln_b, lay):
    n_tok = z.shape[0]
    rows = lay["tseq"]
    hb = rows // CONV_HALO
    n_halo_blocks = n_tok // CONV_HALO
    colb = COL_B // (2 * D_BRANCH)
    seg_tiles = (lay["nc"] // rows, lay["n_ctx"] // rows, lay["s"] // rows)
    vec = lambda a: a.reshape(1, -1)
    return pl.pallas_call(
        functools.partial(_conv_kernel, seg_tiles=seg_tiles),
        out_shape=jax.ShapeDtypeStruct((n_tok, D_BRANCH), BF16),
        grid=(n_tok // rows,),
        in_specs=[
            pl.BlockSpec((rows, 2 * D_BRANCH), lambda i: (i, colb)),
            pl.BlockSpec((CONV_HALO, 2 * D_BRANCH), lambda i: (jnp.maximum(i * hb - 1, 0), colb)),
            pl.BlockSpec((CONV_HALO, 2 * D_BRANCH), lambda i: (jnp.minimum((i + 1) * hb, n_halo_blocks - 1), colb)),
            pl.BlockSpec((CONV_W, D_BRANCH), lambda i: (0, 0)),
            pl.BlockSpec((1, D_BRANCH), lambda i: (0, 0)),
            pl.BlockSpec((1, D_BRANCH), lambda i: (0, 0)),
            pl.BlockSpec((1, D_BRANCH), lambda i: (0, 0)),
        ],
        out_specs=pl.BlockSpec((rows, D_BRANCH), lambda i: (i, 0)),
        scratch_shapes=[pltpu.VMEM((rows + 2 * CONV_HALO, D_BRANCH), F32)],
        compiler_params=_cparams("arbitrary"),
        name="conformer_conv",
    )(z, z, z, w_dw, vec(b_dw), vec(ln_g), vec(ln_b))


def _rope(x, cos, sin_a, sin_b):
    reps = x.shape[1] // cos.shape[1]
    if reps > 1:
        cos, sin_a, sin_b = (jnp.concatenate([t] * reps, axis=1) for t in (cos, sin_a, sin_b))
    n = x.shape[1]
    half = HEAD_DIM // 4
    return x * cos + pltpu.roll(x, n - half, 1) * sin_a + pltpu.roll(x, half, 1) * sin_b


def _attn_core(q, keys, vals, bias, sink_ref, o_ref):
    rows = q.shape[0]
    rid = lax.broadcasted_iota(jnp.int32, (Q_PER_KV * rows, 1), 0)
    for h in range(N_KV_HEADS):
        kh = keys[:, h * HEAD_DIM:(h + 1) * HEAD_DIM]
        vh = vals[:, h * HEAD_DIM:(h + 1) * HEAD_DIM]
        heads = [q[:, (h * Q_PER_KV + g) * HEAD_DIM:(h * Q_PER_KV + g + 1) * HEAD_DIM] for g in range(Q_PER_KV)]
        qh = jnp.concatenate(heads, axis=0).astype(BF16)
        s = lax.dot_general(qh, kh, (((1,), (1,)), ((), ())), preferred_element_type=F32)
        if bias is not None:
            s = s + jnp.concatenate([bias] * Q_PER_KV, axis=0)
        sink = jnp.full((Q_PER_KV * rows, 1), sink_ref[h * Q_PER_KV], F32)
        for g in range(1, Q_PER_KV):
            sink = jnp.where(rid >= g * rows, sink_ref[h * Q_PER_KV + g], sink)
        m = jnp.maximum(jnp.max(s, axis=-1, keepdims=True), sink)
        e = jnp.exp(s - m)
        denom = jnp.sum(e, axis=-1, keepdims=True) + jnp.exp(sink - m)
        o = jnp.dot(e.astype(BF16), vh, preferred_element_type=F32) * (1.0 / denom)
        for g in range(Q_PER_KV):
            c0 = (h * Q_PER_KV + g) * HEAD_DIM
            o_ref[:, c0:c0 + HEAD_DIM] = o[g * rows:(g + 1) * rows, :].astype(o_ref.dtype)


def _attn_latent_kernel(sink_ref, q_ref, kp_ref, kc_ref, kn_ref, vp_ref, vc_ref, vn_ref, kx_ref, vx_ref,
                        cq_ref, cp_ref, cn_ref, o_ref, *, n_blocks):
    n = pl.program_id(1)
    rows = q_ref.shape[0]
    scale = HEAD_DIM ** -0.5

    def tables(ref):
        return ref[0], ref[1], ref[2]

    f32 = lambda ref: ref[...].astype(F32)
    q = _rope(f32(q_ref), *tables(cq_ref)) * scale
    kb = [_rope(f32(kp_ref), *tables(cp_ref)), _rope(f32(kc_ref), *tables(cq_ref)), _rope(f32(kn_ref), *tables(cn_ref))]
    keys = jnp.concatenate([k.astype(BF16) for k in kb] + [kx_ref[...].astype(BF16)], axis=0)
    vals = jnp.concatenate([vp_ref[...].astype(BF16), vc_ref[...].astype(BF16), vn_ref[...].astype(BF16),
                            vx_ref[...].astype(BF16)], axis=0)
    nk = keys.shape[0]
    qi = lax.broadcasted_iota(jnp.int32, (rows, nk), 0)
    kj = lax.broadcasted_iota(jnp.int32, (rows, nk), 1)
    rel = kj - qi
    lo = jnp.where(n > 0, 0, rows)
    hi = jnp.where(n < n_blocks - 1, 3 * rows, 2 * rows)
    ok = (rel >= 0) & (rel <= 2 * WINDOW) & (kj >= lo) & (kj < hi)
    ok = ok | (kj >= 3 * rows)
    bias = jnp.where(ok, 0.0, NEG_INF).astype(F32)
    _attn_core(q, keys, vals, bias, sink_ref, o_ref)


def _attn_ctx_kernel(sink_ref, q_ref, kx_ref, vx_ref, latent_out_ref, o_ref):
    del latent_out_ref
    q = q_ref[...].astype(F32) * (HEAD_DIM ** -0.5)
    _attn_core(q, kx_ref[...].astype(BF16), vx_ref[...].astype(BF16), None, sink_ref, o_ref)


def _rope_tables(s):
    t = jnp.arange(s)
    row = (t // GRID_W).astype(F32)[:, None]
    col = (t % GRID_W).astype(F32)[:, None]
    half = HEAD_DIM // 2
    inv = ROPE_BASE ** (-jnp.arange(0, half, 2, dtype=F32) / half)
    lane = jnp.arange(2 * HEAD_DIM)
    d = lane % HEAD_DIM
    freq = inv[(d % half) % (half // 2)][None, :]
    ang = jnp.where((d < half)[None, :], row * freq, col * freq)
    first = ((d % half) < half // 2)[None, :]
    sin = jnp.sin(ang)
    return jnp.stack([jnp.cos(ang), jnp.where(first, -sin, 0.0), jnp.where(first, 0.0, sin)]).astype(F32)


def _attention(z, sink, rope_tab, lay):
    n_tok = z.shape[0]
    b, s, n_ctx, nc = lay["b"], lay["s"], lay["n_ctx"], lay["nc"]
    nb = s // CHUNK
    cb0 = nc // CHUNK
    kvw = N_KV_HEADS * HEAD_DIM
    qcol, kcol, vcol = COL_Q // D_BRANCH, COL_K // kvw, COL_V // kvw
    lat = lambda bb, n: cb0 + bb * nb + n
    prev = lambda n: jnp.maximum(n - 1, 0)
    nxt = lambda n: jnp.minimum(n + 1, nb - 1)
    smem = pl.BlockSpec(memory_space=pltpu.SMEM)
    out_lat = pl.pallas_call(
        functools.partial(_attn_latent_kernel, n_blocks=nb),
        out_shape=jax.ShapeDtypeStruct((n_tok, D_BRANCH), BF16),
        grid=(b, nb),
        in_specs=[
            smem,
            pl.BlockSpec((CHUNK, D_BRANCH), lambda bb, n: (lat(bb, n), qcol)),
            pl.BlockSpec((CHUNK, kvw), lambda bb, n: (lat(bb, prev(n)), kcol)),
            pl.BlockSpec((CHUNK, kvw), lambda bb, n: (lat(bb, n), kcol)),
            pl.BlockSpec((CHUNK, kvw), lambda bb, n: (lat(bb, nxt(n)), kcol)),
            pl.BlockSpec((CHUNK, kvw), lambda bb, n: (lat(bb, prev(n)), vcol)),
            pl.BlockSpec((CHUNK, kvw), lambda bb, n: (lat(bb, n), vcol)),
            pl.BlockSpec((CHUNK, kvw), lambda bb, n: (lat(bb, nxt(n)), vcol)),
            pl.BlockSpec((n_ctx, kvw), lambda bb, n: (bb, kcol)),
            pl.BlockSpec((n_ctx, kvw), lambda bb, n: (bb, vcol)),
            pl.BlockSpec((3, CHUNK, kvw), lambda bb, n: (0, n, 0)),
            pl.BlockSpec((3, CHUNK, kvw), lambda bb, n: (0, prev(n), 0)),
            pl.BlockSpec((3, CHUNK, kvw), lambda bb, n: (0, nxt(n), 0)),
        ],
        out_specs=pl.BlockSpec((CHUNK, D_BRANCH), lambda bb, n: (lat(bb, n), 0)),
        compiler_params=_cparams("arbitrary", "arbitrary"),
        name="attention_latent",
    )(sink, z, z, z, z, z, z, z, z, z, rope_tab, rope_tab, rope_tab)
    ncb = n_ctx // CHUNK
    return pl.pallas_call(
        _attn_ctx_kernel,
        out_shape=jax.ShapeDtypeStruct((n_tok, D_BRANCH), BF16),
        grid=(b, ncb),
        in_specs=[
            smem,
            pl.BlockSpec((CHUNK, D_BRANCH), lambda bb, n: (bb * ncb + n, qcol)),
            pl.BlockSpec((n_ctx, kvw), lambda bb, n: (bb, kcol)),
            pl.BlockSpec((n_ctx, kvw), lambda bb, n: (bb, vcol)),
            pl.BlockSpec(memory_space=pl.ANY),
        ],
        out_specs=pl.BlockSpec((CHUNK, D_BRANCH), lambda bb, n: (bb * ncb + n, 0)),
        input_output_aliases={4: 0},
        compiler_params=_cparams("arbitrary", "arbitrary"),
        name="attention_ctx",
    )(sink, z, z, z, out_lat)


def _s5_tables(a_re, a_im, log_step, b_re, b_im, c_re, c_im):
    t_len = CHUNK
    hi = lax.Precision.HIGHEST
    j = jnp.arange(t_len + 1, dtype=F32)
    lam = lax.complex(a_re.astype(F32), a_im.astype(F32))
    log_lbar = lam * jnp.exp(log_step.astype(F32))[..., None]
    lbar = jnp.exp(log_lbar)
    b = lax.complex(b_re.astype(F32), b_im.astype(F32))
    bbar = ((lbar - 1.0) / lam)[..., None] * b[None]
    cc = lax.complex(c_re.astype(F32), c_im.astype(F32))
    pw = jnp.exp(log_lbar[:, :, None, :] * j[None, None, :, None])
    pw_re, pw_im = pw.real, pw.imag

    q = cc[:, :, :, None, :] * bbar.transpose(0, 1, 3, 2)[:, :, None, :, :]
    taps = (jnp.einsum("dgcip,dgjp->dgjci", q.real, pw_re[:, :, :t_len], precision=hi)
            - jnp.einsum("dgcip,dgjp->dgjci", q.imag, pw_im[:, :, :t_len], precision=hi))
    kf, kb = taps[0], taps[1]
    kc = jnp.concatenate([kb[:, :0:-1], (kf[:, :1] + kb[:, :1]), kf[:, 1:]], axis=1)
    strip = kc.transpose(0, 3, 1, 2).reshape(S5_GROUPS, S5_GW, (2 * t_len - 1) * S5_GW)
    width = 2 * t_len * S5_GW
    strip = jnp.pad(strip, ((0, 0), (0, 0), (0, width + 128 - strip.shape[-1])))

    def lanes4(f, bk):
        return jnp.concatenate([f, bk, f, bk], axis=-1)

    bt = bbar.transpose(0, 1, 3, 2)
    p1 = lanes4(pw_re[0, :, t_len - 1::-1], pw_re[1, :, :t_len])
    p2 = lanes4(pw_im[0, :, t_len - 1::-1], pw_im[1, :, :t_len])
    x1 = jnp.concatenate([bt[0].real, bt[1].real, bt[0].imag, bt[1].imag], axis=-1)
    x2 = jnp.concatenate([-bt[0].imag, -bt[1].imag, bt[0].real, bt[1].real], axis=-1)
    q1 = lanes4(pw_re[0, :, 1:t_len + 1], pw_re[1, :, t_len:0:-1])
    q2 = lanes4(pw_im[0, :, 1:t_len + 1], pw_im[1, :, t_len:0:-1])
    y1 = jnp.concatenate([cc[0].real, cc[1].real, -cc[0].imag, -cc[1].imag], axis=-1)
    y2 = jnp.concatenate([-cc[0].imag, -cc[1].imag, -cc[0].real, -cc[1].real], axis=-1)
    rows = jnp.stack([p1, p2, q1, q2], axis=1)
    cols = jnp.stack([x1, x2, y1, y2], axis=1)

    dre = jnp.concatenate([pw_re[0, :, t_len], pw_re[1, :, t_len]], axis=-1)
    dim = jnp.concatenate([pw_im[0, :, t_len], pw_im[1, :, t_len]], axis=-1)
    decay = jnp.stack([dre, dim], axis=1)
    return strip.astype(F32), rows.astype(F32), cols.astype(F32), decay.astype(F32)


def _s5_kernel(u_ref, strip_ref, rows_ref, cols_ref, dec_ref, dskip_ref, y_ref,
               mt_ref, strips_ref, wst_ref, clt_ref, dsre_ref, dsim_ref, fre_ref, fim_ref, bre_ref, bim_ref,
               *, batch, n_ctx_steps, n_lat_steps):
    t_len = CHUNK
    width = t_len * S5_GW
    lanes = 2 * S5_STATE
    strip = strip_ref[0]
    for r in range(8):
        strips_ref[r] = strip[:, r * S5_GW:r * S5_GW + 2 * width].astype(BF16)
    for s in range(t_len):
        off = (t_len - 1 - s) * S5_GW
        r, q = (off // S5_GW) % 8, off // 128
        mt_ref[s * S5_GW:(s + 1) * S5_GW, :] = strips_ref[r, :, q * 128:q * 128 + width]
    for dst_ref, k in ((wst_ref, 0), (clt_ref, 2)):
        col_a, col_b = cols_ref[0, k], cols_ref[0, k + 1]
        for s in range(t_len):
            blk = col_a * rows_ref[0, k, s:s + 1, :] + col_b * rows_ref[0, k + 1, s:s + 1, :]
            dst_ref[s * S5_GW:(s + 1) * S5_GW, :] = blk.astype(BF16)
    u = u_ref[0]
    ds = jnp.dot(u, wst_ref[...], preferred_element_type=F32)
    dsre_ref[...] = ds[:, :lanes]
    dsim_ref[...] = ds[:, lanes:]

    dre = jnp.broadcast_to(dec_ref[0, 0:1, :], (batch, lanes))
    dim = jnp.broadcast_to(dec_ref[0, 1:2, :], (batch, lanes))
    is_fwd = lax.broadcasted_iota(jnp.int32, (batch, lanes), 1) < S5_STATE

    def run(first_row, n_steps, carry):
        def step(it, carry):
            sre, sim = carry
            rows_f = pl.ds(first_row + it, batch, stride=n_steps)
            rows_b = pl.ds(first_row + n_steps - 1 - it, batch, stride=n_steps)
            fre_ref[rows_f, :] = sre
            fim_ref[rows_f, :] = sim
            bre_ref[rows_b, :] = sre
            bim_ref[rows_b, :] = sim
            add_re = jnp.where(is_fwd, dsre_ref[rows_f, :], dsre_ref[rows_b, :])
            add_im = jnp.where(is_fwd, dsim_ref[rows_f, :], dsim_ref[rows_b, :])
            return sre * dre - sim * dim + add_re, sre * dim + sim * dre + add_im

        return lax.fori_loop(0, n_steps, step, carry)

    zero = jnp.zeros((batch, lanes), F32)
    carry = run(0, n_ctx_steps, (zero, zero))
    run(batch * n_ctx_steps, n_lat_steps, carry)

    n_rows = dsre_ref.shape[0]
    fwd_rows = lax.broadcasted_iota(jnp.int32, (n_rows, lanes), 1) < S5_STATE
    sin = jnp.concatenate([jnp.where(fwd_rows, fre_ref[...], bre_ref[...]),
                           jnp.where(fwd_rows, fim_ref[...], bim_ref[...])], axis=1).astype(BF16)
    y = jnp.dot(u, mt_ref[...], preferred_element_type=F32)
    y = y + lax.dot_general(sin, clt_ref[...], (((1,), (1,)), ((), ())), preferred_element_type=F32)
    y_ref[0] = _gelu_tanh(y + dskip_ref[0] * u.astype(F32)).astype(y_ref.dtype)


def _s5_mix(ud, tables, d_skip, lay):
    b, s, n_ctx = lay["b"], lay["s"], lay["n_ctx"]
    strip, rows, cols, decay = tables
    width = CHUNK * S5_GW
    ncs, nls = n_ctx // CHUNK, s // CHUNK
    m = b * (ncs + nls)
    ug = ud.reshape(m, CHUNK, S5_GROUPS, S5_GW).transpose(2, 0, 1, 3).reshape(S5_GROUPS, m, width)
    dskip = jnp.broadcast_to(d_skip.astype(F32).reshape(S5_GROUPS, 1, 1, S5_GW),
                             (S5_GROUPS, 1, CHUNK, S5_GW)).reshape(S5_GROUPS, 1, width)
    state_rows = pltpu.VMEM((m, 2 * S5_STATE), F32)
    y = pl.pallas_call(
        functools.partial(_s5_kernel, batch=b, n_ctx_steps=ncs, n_lat_steps=nls),
        out_shape=jax.ShapeDtypeStruct((S5_GROUPS, m, width), BF16),
        grid=(S5_GROUPS,),
        in_specs=[
            pl.BlockSpec((1, m, width), lambda g: (g, 0, 0)),
            pl.BlockSpec((1, S5_GW, 2 * width + 128), lambda g: (g, 0, 0)),
            pl.BlockSpec((1, 4, CHUNK, 4 * S5_STATE), lambda g: (g, 0, 0, 0)),
            pl.BlockSpec((1, 4, S5_GW, 4 * S5_STATE), lambda g: (g, 0, 0, 0)),
            pl.BlockSpec((1, 2, 2 * S5_STATE), lambda g: (g, 0, 0)),
            pl.BlockSpec((1, 1, width), lambda g: (g, 0, 0)),
        ],
        out_specs=pl.BlockSpec((1, m, width), lambda g: (g, 0, 0)),
        scratch_shapes=[
            pltpu.VMEM((width, width), BF16),
            pltpu.VMEM((8, S5_GW, 2 * width), BF16),
            pltpu.VMEM((width, 4 * S5_STATE), BF16),
            pltpu.VMEM((width, 4 * S5_STATE), BF16),
            state_rows, state_rows, state_rows, state_rows, state_rows, state_rows,
        ],
        compiler_params=_cparams("arbitrary"),
        name="s5_scan",
    )(ug, strip, rows, cols, decay, dskip)
    return y.reshape(S5_GROUPS, m, CHUNK, S5_GW).transpose(1, 2, 0, 3).reshape(m * CHUNK, D_BRANCH)


def _s5_readout_kernel(y_ref, w_ref, o_ref):
    r = jnp.dot(y_ref[...], w_ref[...], preferred_element_type=F32)
    o_ref[...] = (r[:, :D_BRANCH] * _sigmoid(r[:, D_BRANCH:])).astype(o_ref.dtype)


def _s5_readout(y, w_glu, layer, lay):
    n_tok = y.shape[0]
    rows = lay["tm"]
    return pl.pallas_call(
        _s5_readout_kernel,
        out_shape=jax.ShapeDtypeStruct((n_tok, D_BRANCH), BF16),
        grid=(n_tok // rows,),
        in_specs=[
            pl.BlockSpec((rows, D_BRANCH), lambda i: (i, 0)),
            pl.BlockSpec((None, D_BRANCH, 2 * D_BRANCH), lambda i: (layer, 0, 0)),
        ],
        out_specs=pl.BlockSpec((rows, D_BRANCH), lambda i: (i, 0)),
        compiler_params=_cparams("arbitrary"),
        name="s5_readout",
    )(y, w_glu)


def _merge_kernel(h_ref, ba_ref, bb_ref, bc_ref, bd_ref, wg_ref, bg_ref, wb_ref, o_ref):
    h = h_ref[...]
    acc = None
    for k, br_ref in enumerate((ba_ref, bb_ref, bc_ref, bd_ref)):
        gate = _sigmoid(jnp.dot(h, wg_ref[k], preferred_element_type=F32) + bg_ref[k])
        term = gate * jnp.dot(br_ref[...], wb_ref[k], preferred_element_type=F32)
        acc = term if acc is None else acc + term
    o_ref[...] = acc.astype(o_ref.dtype)


def _merge(h, branches, w_gate, b_gate, w_branch, layer, first, lay):
    n_tok, d = h.shape
    tm, tn = lay["tm"], 512
    br_spec = pl.BlockSpec((tm, D_BRANCH), lambda i, j: (i + first, 0))
    return pl.pallas_call(
        _merge_kernel,
        out_shape=jax.ShapeDtypeStruct((n_tok, d), BF16),
        grid=(n_tok // tm - first, d // tn),
        in_specs=[
            pl.BlockSpec((tm, d), lambda i, j: (i + first, 0)),
            br_spec, br_spec, br_spec, br_spec,
            pl.BlockSpec((None, N_BRANCH, d, tn), lambda i, j: (layer, 0, 0, j)),
            pl.BlockSpec((N_BRANCH, 1, tn), lambda i, j: (0, 0, j)),
            pl.BlockSpec((None, N_BRANCH, D_BRANCH, tn), lambda i, j: (layer, 0, 0, j)),
        ],
        out_specs=pl.BlockSpec((tm, tn), lambda i, j: (i + first, j)),
        compiler_params=_cparams("arbitrary", "arbitrary"),
        name="branch_merge",
    )(h, *branches, w_gate, b_gate.reshape(N_BRANCH, 1, d), w_branch)


def _outproj_kernel(m_ref, w_ref, x_ref, gate_ref, o_ref):
    o_ref[...] = x_ref[...] + gate_ref[0] * jnp.dot(m_ref[...], w_ref[...], preferred_element_type=F32)


def _out_projection(merged, w_out, xall, mod3, layer, first, lay):
    n_tok, d = xall.shape
    split = 2
    tm = lay["tm"] // split
    mod_row = lay["mod_row"]
    skip = first * split
    return pl.pallas_call(
        _outproj_kernel,
        out_shape=jax.ShapeDtypeStruct((n_tok, d), F32),
        grid=(n_tok // tm - skip,),
        in_specs=[
            pl.BlockSpec((tm, d), lambda i: (i + skip, 0)),
            pl.BlockSpec((None, d, d), lambda i: (layer, 0, 0)),
            pl.BlockSpec((tm, d), lambda i: (i + skip, 0)),
            pl.BlockSpec((1, 1, d), lambda i: (mod_row((i + skip) // split, 2), 0, 0)),
        ],
        out_specs=pl.BlockSpec((tm, d), lambda i: (i + skip, 0)),
        compiler_params=_cparams("arbitrary"),
        name="out_projection",
    )(merged, w_out, xall, mod3)


def _ffn_kernel(x_ref, g_ref, shift_ref, scale_ref, gate_ref, w1_ref, w2_ref, o_ref, h_ref):
    j = pl.program_id(1)

    @pl.when(j == 0)
    def _():
        _modulated_rms_norm(x_ref, g_ref, shift_ref, scale_ref, h_ref)
        o_ref[...] = jnp.zeros_like(o_ref)

    a = jnp.maximum(jnp.dot(h_ref[...], w1_ref[...], preferred_element_type=F32), 0.0)
    o_ref[...] += jnp.dot((a * a).astype(BF16), w2_ref[...], preferred_element_type=F32)

    @pl.when(j == pl.num_programs(1) - 1)
    def _():
        o_ref[...] = x_ref[...] + gate_ref[0] * o_ref[...]


def _ffn(xall, norm_g, mod3, w1, w2, layer, first, lay):
    n_tok, d = xall.shape
    tm, tf = lay["tm"], 512
    mod_row = lay["mod_row"]
    mspec = lambda which: pl.BlockSpec((1, 1, d), lambda i, j: (mod_row(i + first, which), 0, 0))
    return pl.pallas_call(
        _ffn_kernel,
        out_shape=jax.ShapeDtypeStruct((n_tok, d), F32),
        grid=(n_tok // tm - first, D_FF // tf),
        in_specs=[
            pl.BlockSpec((tm, d), lambda i, j: (i + first, 0)),
            pl.BlockSpec((1, d), lambda i, j: (0, 0)),
            mspec(3), mspec(4), mspec(5),
            pl.BlockSpec((None, d, tf), lambda i, j: (layer, 0, j)),
            pl.BlockSpec((None, tf, d), lambda i, j: (layer, j, 0)),
        ],
        out_specs=pl.BlockSpec((tm, d), lambda i, j: (i + first, 0)),
        scratch_shapes=[pltpu.VMEM((tm, d), BF16)],
        compiler_params=_cparams("arbitrary", "arbitrary"),
        name="ffn",
    )(xall, norm_g.reshape(1, d), mod3, mod3, mod3, w1, w2)


def _final_norm_kernel(x_ref, g_ref, o_ref):
    x = x_ref[...]
    o_ref[...] = x * lax.rsqrt(jnp.mean(x * x, axis=-1, keepdims=True) + EPS) * g_ref[...]


def _final_norm(xall, g, lay):
    d = xall.shape[1]
    tm = lay["tm"]
    first = lay["nc"] // tm
    n_lat = lay["b"] * lay["s"]
    return pl.pallas_call(
        _final_norm_kernel,
        out_shape=jax.ShapeDtypeStruct((n_lat, d), F32),
        grid=(n_lat // tm,),
        in_specs=[pl.BlockSpec((tm, d), lambda i: (i + first, 0)), pl.BlockSpec((1, d), lambda i: (0, 0))],
        out_specs=pl.BlockSpec((tm, d), lambda i: (i, 0)),
        compiler_params=_cparams("arbitrary"),
        name="final_norm",
    )(xall, g.reshape(1, d))


def _reorder_in_columns(w_in):
    q_end = 5 * D_BRANCH
    kv = 2 * N_KV_HEADS * HEAD_DIM
    pad = jnp.zeros(w_in.shape[:-1] + (IN_COLS_PAD - IN_COLS,), w_in.dtype)
    return jnp.concatenate([w_in[..., :q_end], w_in[..., q_end + kv:], w_in[..., q_end:q_end + kv], pad], axis=-1)


def kernel(x, c, ctx, c_ctx, w_mod, b_mod, norm1_g, norm2_g, w_in, gmlp_ln_g, gmlp_ln_b, gmlp_ws, gmlp_bs, conv_w, conv_b, conv_ln_g, conv_ln_b, attn_sink, s5_a_re, s5_a_im, s5_log_step, s5_b_re, s5_b_im, s5_c_re, s5_c_im, s5_d, s5_w_glu, w_branch, w_gate, b_gate, w_out, w_ff1, w_ff2, final_g):
    b, s, d = x.shape
    n_ctx = ctx.shape[1]
    depth = w_mod.shape[0]
    assert d == D_MODEL and s % GRID_W == 0
    nc = b * n_ctx
    tm = _row_tile(nc, s)
    tseq = min(256, tm)
    nct, lpt = nc // tm, s // tm

    def mod_row(i, which):
        bidx = jnp.where(i < nct, b, (i - nct) // lpt)
        return bidx * N_MOD + which

    lay = dict(b=b, s=s, n_ctx=n_ctx, nc=nc, tm=tm, tseq=tseq, mod_row=mod_row)

    cond = jnp.concatenate([c, c_ctx[None, :]], axis=0)
    cond = jnp.pad(cond, ((0, -cond.shape[0] % 8), (0, 0)))
    mod_all = _modulation(cond, w_mod, b_mod)
    rope_tab = _rope_tables(s)
    xall = jnp.concatenate([ctx.reshape(nc, d), x.reshape(b * s, d)], axis=0)

    w_in_b = _reorder_in_columns(w_in).astype(BF16)
    w_gate_b, w_branch_b, w_out_b = w_gate.astype(BF16), w_branch.astype(BF16), w_out.astype(BF16)
    w_ff1_b, w_ff2_b, w_glu_b = w_ff1.astype(BF16), w_ff2.astype(BF16), s5_w_glu.astype(BF16)

    for l in range(depth):
        first = nct if l == depth - 1 else 0
        mod3 = mod_all[l, :b + 1].reshape((b + 1) * N_MOD, 1, d)
        h, z, ud = _in_projection(xall, norm1_g[l], mod3, w_in_b, l, lay)
        br_a = _gmlp(z, gmlp_ln_g[l], gmlp_ln_b[l], gmlp_ws[l], gmlp_bs[l], lay)
        br_b = _conformer_conv(z, conv_w[l], conv_b[l], conv_ln_g[l], conv_ln_b[l], lay)
        br_c = _attention(z, attn_sink[l], rope_tab, lay)
        tables = _s5_tables(s5_a_re[l], s5_a_im[l], s5_log_step[l], s5_b_re[l], s5_b_im[l], s5_c_re[l], s5_c_im[l])
        y = _s5_mix(ud, tables, s5_d[l], lay)
        br_d = _s5_readout(y, w_glu_b, l, lay)
        merged = _merge(h, (br_a, br_b, br_c, br_d), w_gate_b, b_gate[l], w_branch_b, l, first, lay)
        xall = _out_projection(merged, w_out_b, xall, mod3, l, first, lay)
        xall = _ffn(xall, norm2_g[l], mod3, w_ff1_b, w_ff2_b, l, first, lay)

    return _final_norm(xall, final_g, lay).reshape(b, s, d)
```

```python
import functools
import math

import jax
import jax.numpy as jnp
from jax import lax
from jax.experimental import pallas as pl
from jax.experimental.pallas import tpu as pltpu

D_MODEL = 2048
N_BRANCH = 4
D_BRANCH = D_MODEL // N_BRANCH
CHUNK = 128
GMLP_GROUPS = 4
CONV_W = 31
CONV_HALO = 16
HEAD_DIM = 64
N_Q_HEADS = D_BRANCH // HEAD_DIM
N_KV_HEADS = 2
Q_PER_KV = N_Q_HEADS // N_KV_HEADS
WINDOW = 128
ROPE_BASE = 10000.0
GRID_W = 64
S5_GW = 16
S5_GROUPS = D_BRANCH // S5_GW
S5_STATE = 64
D_FF = 4 * D_MODEL
N_MOD = 6
EPS = 1e-6
NEG_INF = -1e30

COL_A = 0
COL_B = 2 * D_BRANCH
COL_Q = 4 * D_BRANCH
COL_D = 5 * D_BRANCH
COL_K = 6 * D_BRANCH
COL_V = COL_K + N_KV_HEADS * HEAD_DIM
IN_COLS = COL_V + N_KV_HEADS * HEAD_DIM
IN_COLS_PAD = -(-IN_COLS // D_BRANCH) * D_BRANCH

VMEM_LIMIT_V7X = 56 * 1024 * 1024

F32 = jnp.float32
BF16 = jnp.bfloat16


def _cparams(*sem):
    return pltpu.CompilerParams(dimension_semantics=sem, vmem_limit_bytes=VMEM_LIMIT_V7X)


def _gelu_tanh(x):
    return 0.5 * x * (1.0 + jnp.tanh(math.sqrt(2.0 / math.pi) * (x + 0.044715 * (x * x * x))))


def _sigmoid(x):
    return 1.0 / (1.0 + jnp.exp(-x))


def _layer_norm_rows(v, g, b):
    mu = jnp.mean(v, axis=-1, keepdims=True)
    vc = v - mu
    var = jnp.mean(vc * vc, axis=-1, keepdims=True)
    return vc * lax.rsqrt(var + EPS) * g + b


def _row_tile(nc, s):
    for t in (1024, 512, 256, 128):
        if nc % t == 0 and s % t == 0:
            return t
    raise ValueError("context and sequence lengths must be multiples of 128")


def _mod_kernel(c_ref, w_ref, b_ref, o_ref):
    c = c_ref[...]
    c = (c * _sigmoid(c)).astype(BF16)
    o_ref[0] = jnp.dot(c, w_ref[0].astype(BF16), preferred_element_type=F32) + b_ref[0]


def _modulation(cond, w_mod, b_mod):
    depth, d, n = w_mod.shape
    rows = cond.shape[0]
    tn = 1024
    return pl.pallas_call(
        _mod_kernel,
        out_shape=jax.ShapeDtypeStruct((depth, rows, n), F32),
        grid=(depth, n // tn),
        in_specs=[
            pl.BlockSpec((rows, d), lambda l, j: (0, 0)),
            pl.BlockSpec((1, d, tn), lambda l, j: (l, 0, j)),
            pl.BlockSpec((1, 1, tn), lambda l, j: (l, 0, j)),
        ],
        out_specs=pl.BlockSpec((1, rows, tn), lambda l, j: (l, 0, j)),
        compiler_params=_cparams("arbitrary", "arbitrary"),
        name="modulation",
    )(cond, w_mod, b_mod.reshape(depth, 1, n))


NORM_ROWS = 128


def _modulated_rms_norm(x_ref, g_ref, shift_ref, scale_ref, h_ref):
    gain = g_ref[...] * (1.0 + scale_ref[0])
    shift = shift_ref[0]

    def body(r, carry):
        rows = pl.ds(pl.multiple_of(r * NORM_ROWS, NORM_ROWS), NORM_ROWS)
        x = x_ref[rows, :]
        inv = lax.rsqrt(jnp.mean(x * x, axis=-1, keepdims=True) + EPS)
        h_ref[rows, :] = (x * inv * gain + shift).astype(h_ref.dtype)
        return carry

    lax.fori_loop(0, x_ref.shape[0] // NORM_ROWS, body, 0)


def _inproj_kernel(x_ref, g_ref, shift_ref, scale_ref, w_ref, h_ref, z_ref, ud_ref):
    _modulated_rms_norm(x_ref, g_ref, shift_ref, scale_ref, h_ref)
    h = h_ref[...]
    for c0 in range(0, z_ref.shape[1], D_BRANCH):
        z = jnp.dot(h, w_ref[:, c0:c0 + D_BRANCH], preferred_element_type=F32).astype(z_ref.dtype)
        z_ref[:, c0:c0 + D_BRANCH] = z
        if c0 == COL_D:
            ud_ref[...] = z


def _in_projection(xall, norm_g, mod3, w_in, layer, lay):
    n_tok, d = xall.shape
    split = 2
    tm = lay["tm"] // split
    mod_row = lay["mod_row"]
    return pl.pallas_call(
        _inproj_kernel,
        out_shape=(jax.ShapeDtypeStruct((n_tok, d), BF16), jax.ShapeDtypeStruct((n_tok, IN_COLS_PAD), BF16),
                   jax.ShapeDtypeStruct((n_tok, D_BRANCH), BF16)),
        grid=(n_tok // tm,),
        in_specs=[
            pl.BlockSpec((tm, d), lambda i: (i, 0)),
            pl.BlockSpec((1, d), lambda i: (0, 0)),
            pl.BlockSpec((1, 1, d), lambda i: (mod_row(i // split, 0), 0, 0)),
            pl.BlockSpec((1, 1, d), lambda i: (mod_row(i // split, 1), 0, 0)),
            pl.BlockSpec((None, d, IN_COLS_PAD), lambda i: (layer, 0, 0)),
        ],
        out_specs=(pl.BlockSpec((tm, d), lambda i: (i, 0)), pl.BlockSpec((tm, IN_COLS_PAD), lambda i: (i, 0)),
                   pl.BlockSpec((tm, D_BRANCH), lambda i: (i, 0))),
        compiler_params=_cparams("arbitrary"),
        name="in_projection",
    )(xall, norm_g.reshape(1, d), mod3, mod3, w_in)


def _gmlp_kernel(za_ref, lng_ref, lnb_ref, ws_ref, bs_ref, o_ref):
    rows = za_ref.shape[0]
    gw = D_BRANCH // GMLP_GROUPS
    a = _gelu_tanh(za_ref[...].astype(F32))
    u = a[:, :D_BRANCH]
    v = _layer_norm_rows(a[:, D_BRANCH:], lng_ref[...], lnb_ref[...]).astype(BF16)
    for c in range(rows // CHUNK):
        r0 = c * CHUNK
        for g in range(GMLP_GROUPS):
            c0 = g * gw
            mixed = jnp.dot(ws_ref[g], v[r0:r0 + CHUNK, c0:c0 + gw], preferred_element_type=F32) + bs_ref[g]
            o_ref[r0:r0 + CHUNK, c0:c0 + gw] = (u[r0:r0 + CHUNK, c0:c0 + gw] * mixed).astype(o_ref.dtype)


def _gmlp(z, ln_g, ln_b, ws, bs, lay):
    n_tok = z.shape[0]
    rows = min(512, lay["tm"])
    gw = D_BRANCH // GMLP_GROUPS
    bs_full = jnp.broadcast_to(bs[:, :, None], (GMLP_GROUPS, CHUNK, gw)).astype(F32)
    return pl.pallas_call(
        _gmlp_kernel,
        out_shape=jax.ShapeDtypeStruct((n_tok, D_BRANCH), BF16),
        grid=(n_tok // rows,),
        in_specs=[
            pl.BlockSpec((rows, 2 * D_BRANCH), lambda i: (i, COL_A // (2 * D_BRANCH))),
            pl.BlockSpec((1, D_BRANCH), lambda i: (0, 0)),
            pl.BlockSpec((1, D_BRANCH), lambda i: (0, 0)),
            pl.BlockSpec((GMLP_GROUPS, CHUNK, CHUNK), lambda i: (0, 0, 0)),
            pl.BlockSpec((GMLP_GROUPS, CHUNK, gw), lambda i: (0, 0, 0)),
        ],
        out_specs=pl.BlockSpec((rows, D_BRANCH), lambda i: (i, 0)),
        compiler_params=_cparams("arbitrary"),
        name="gmlp_mix",
    )(z, ln_g.reshape(1, -1), ln_b.reshape(1, -1), ws.astype(BF16), bs_full)


def _conv_kernel(zm_ref, zp_ref, zn_ref, w_ref, cb_ref, lng_ref, lnb_ref, o_ref, ybuf, *, seg_tiles):
    rows = zm_ref.shape[0]
    i = pl.program_id(0)
    nct, cpt, lpt = seg_tiles
    pos = jnp.where(i < nct, i % cpt, (i - nct) % lpt)
    seg = jnp.where(i < nct, cpt, lpt)
    keep_prev = (pos != 0).astype(F32)
    keep_next = (pos != seg - 1).astype(F32)

    def glu(ref):
        z = ref[...].astype(F32)
        return z[:, :D_BRANCH] * _sigmoid(z[:, D_BRANCH:])

    ybuf[0:CONV_HALO, :] = glu(zp_ref) * keep_prev
    ybuf[CONV_HALO:CONV_HALO + rows, :] = glu(zm_ref)
    ybuf[CONV_HALO + rows:, :] = glu(zn_ref) * keep_next

    sub, tile = 32, 8
    off = CONV_HALO - CONV_W // 2
    for r in range(rows // sub):
        acc = jnp.zeros((sub, D_BRANCH), F32)
        for res in range(tile):
            part = None
            for k in range(CONV_W):
                if (off + k) % tile != res:
                    continue
                start = r * sub + off + k - res
                term = ybuf[start:start + sub + tile, :] * w_ref[k:k + 1, :]
                part = term if part is None else part + term
            if part is not None:
                acc = acc + part[res:res + sub, :]
        y = _layer_norm_rows(acc + cb_ref[...], lng_ref[...], lnb_ref[...])
        o_ref[r * sub:(r + 1) * sub, :] = (y * _sigmoid(y)).astype(o_ref.dtype)


def _conformer_conv(z, w_dw, b_dw, ln_g, ln_b, lay):
    n_tok = z.shape[0]
    rows = lay["tseq"]
    hb = rows // CONV_HALO
    n_halo_blocks = n_tok // CONV_HALO
    colb = COL_B // (2 * D_BRANCH)
    seg_tiles = (lay["nc"] // rows, lay["n_ctx"] // rows, lay["s"] // rows)
    vec = lambda a: a.reshape(1, -1)
    return pl.pallas_call(
        functools.partial(_conv_kernel, seg_tiles=seg_tiles),
        out_shape=jax.ShapeDtypeStruct((n_tok, D_BRANCH), BF16),
        grid=(n_tok // rows,),
        in_specs=[
            pl.BlockSpec((rows, 2 * D_BRANCH), lambda i: (i, colb)),
            pl.BlockSpec((CONV_HALO, 2 * D_BRANCH), lambda i: (jnp.maximum(i * hb - 1, 0), colb)),
            pl.BlockSpec((CONV_HALO, 2 * D_BRANCH), lambda i: (jnp.minimum((i + 1) * hb, n_halo_blocks - 1), colb)),
            pl.BlockSpec((CONV_W, D_BRANCH), lambda i: (0, 0)),
            pl.BlockSpec((1, D_BRANCH), lambda i: (0, 0)),
            pl.BlockSpec((1, D_BRANCH), lambda i: (0, 0)),
            pl.BlockSpec((1, D_BRANCH), lambda i: (0, 0)),
        ],
        out_specs=pl.BlockSpec((rows, D_BRANCH), lambda i: (i, 0)),
        scratch_shapes=[pltpu.VMEM((rows + 2 * CONV_HALO, D_BRANCH), F32)],
        compiler_params=_cparams("arbitrary"),
        name="conformer_conv",
    )(z, z, z, w_dw, vec(b_dw), vec(ln_g), vec(ln_b))


def _rope(x, cos, sin_a, sin_b):
    reps = x.shape[1] // cos.shape[1]
    if reps > 1:
        cos, sin_a, sin_b = (jnp.concatenate([t] * reps, axis=1) for t in (cos, sin_a, sin_b))
    n = x.shape[1]
    half = HEAD_DIM // 4
    return x * cos + pltpu.roll(x, n - half, 1) * sin_a + pltpu.roll(x, half, 1) * sin_b


def _attn_core(q, keys, vals, bias, sink_ref, o_ref, r0=0):
    rows = q.shape[0]
    rid = lax.broadcasted_iota(jnp.int32, (Q_PER_KV * rows, 1), 0)
    for h in range(N_KV_HEADS):
        kh = keys[:, h * HEAD_DIM:(h + 1) * HEAD_DIM]
        vh = vals[:, h * HEAD_DIM:(h + 1) * HEAD_DIM]
        heads = [q[:, (h * Q_PER_KV + g) * HEAD_DIM:(h * Q_PER_KV + g + 1) * HEAD_DIM] for g in range(Q_PER_KV)]
        qh = jnp.concatenate(heads, axis=0).astype(BF16)
        s = lax.dot_general(qh, kh, (((1,), (1,)), ((), ())), preferred_element_type=F32)
        if bias is not None:
            s = s + jnp.concatenate([bias] * Q_PER_KV, axis=0)
        sink = jnp.full((Q_PER_KV * rows, 1), sink_ref[h * Q_PER_KV], F32)
        for g in range(1, Q_PER_KV):
            sink = jnp.where(rid >= g * rows, sink_ref[h * Q_PER_KV + g], sink)
        m = jnp.maximum(jnp.max(s, axis=-1, keepdims=True), sink)
        e = jnp.exp(s - m)
        denom = jnp.sum(e, axis=-1, keepdims=True) + jnp.exp(sink - m)
        o = jnp.dot(e.astype(BF16), vh, preferred_element_type=F32) * (1.0 / denom)
        for g in range(Q_PER_KV):
            c0 = (h * Q_PER_KV + g) * HEAD_DIM
            o_ref[r0:r0 + rows, c0:c0 + HEAD_DIM] = o[g * rows:(g + 1) * rows, :].astype(o_ref.dtype)


def _attn_latent_kernel(sink_ref, q_ref, kp_ref, kc_ref, kn_ref, vp_ref, vc_ref, vn_ref, kx_ref, vx_ref,
                        cq_ref, cp_ref, cn_ref, o_ref, *, n_blocks):
    m = pl.program_id(1)
    rows = CHUNK
    scale = HEAD_DIM ** -0.5

    def tables(ref):
        return ref[0], ref[1], ref[2]

    f32 = lambda ref: ref[...].astype(F32)
    q = _rope(f32(q_ref), *tables(cq_ref)) * scale
    k_cur = _rope(f32(kc_ref), *tables(cq_ref)).astype(BF16)
    k_blocks = [_rope(f32(kp_ref), *tables(cp_ref)).astype(BF16), k_cur[:rows], k_cur[rows:],
                _rope(f32(kn_ref), *tables(cn_ref)).astype(BF16)]
    v_cur = vc_ref[...].astype(BF16)
    v_blocks = [vp_ref[...].astype(BF16), v_cur[:rows], v_cur[rows:], vn_ref[...].astype(BF16)]
    kx, vx = kx_ref[...].astype(BF16), vx_ref[...].astype(BF16)
    nk = 3 * rows + kx.shape[0]
    qi = lax.broadcasted_iota(jnp.int32, (rows, nk), 0)
    kj = lax.broadcasted_iota(jnp.int32, (rows, nk), 1)
    rel = kj - qi
    for half in range(2):
        n = 2 * m + half
        keys = jnp.concatenate(k_blocks[half:half + 3] + [kx], axis=0)
        vals = jnp.concatenate(v_blocks[half:half + 3] + [vx], axis=0)
        lo = jnp.where(n > 0, 0, rows)
        hi = jnp.where(n < n_blocks - 1, 3 * rows, 2 * rows)
        ok = (rel >= 0) & (rel <= 2 * WINDOW) & (kj >= lo) & (kj < hi)
        ok = ok | (kj >= 3 * rows)
        bias = jnp.where(ok, 0.0, NEG_INF).astype(F32)
        _attn_core(q[half * rows:(half + 1) * rows], keys, vals, bias, sink_ref, o_ref, r0=half * rows)


def _attn_ctx_kernel(sink_ref, q_ref, kx_ref, vx_ref, latent_out_ref, o_ref):
    del latent_out_ref
    q = q_ref[...].astype(F32) * (HEAD_DIM ** -0.5)
    _attn_core(q, kx_ref[...].astype(BF16), vx_ref[...].astype(BF16), None, sink_ref, o_ref)


def _rope_tables(s):
    t = jnp.arange(s)
    row = (t // GRID_W).astype(F32)[:, None]
    col = (t % GRID_W).astype(F32)[:, None]
    half = HEAD_DIM // 2
    inv = ROPE_BASE ** (-jnp.arange(0, half, 2, dtype=F32) / half)
    lane = jnp.arange(2 * HEAD_DIM)
    d = lane % HEAD_DIM
    freq = inv[(d % half) % (half // 2)][None, :]
    ang = jnp.where((d < half)[None, :], row * freq, col * freq)
    first = ((d % half) < half // 2)[None, :]
    sin = jnp.sin(ang)
    return jnp.stack([jnp.cos(ang), jnp.where(first, -sin, 0.0), jnp.where(first, 0.0, sin)]).astype(F32)


def _attention(z, sink, rope_tab, lay):
    n_tok = z.shape[0]
    b, s, n_ctx, nc = lay["b"], lay["s"], lay["n_ctx"], lay["nc"]
    nb = s // CHUNK
    cb0 = nc // CHUNK
    kvw = N_KV_HEADS * HEAD_DIM
    qcol, kcol, vcol = COL_Q // D_BRANCH, COL_K // kvw, COL_V // kvw
    assert nb % 2 == 0 and cb0 % 2 == 0
    pair = 2 * CHUNK
    lat = lambda bb, n: cb0 + bb * nb + n
    lat2 = lambda bb, m: cb0 // 2 + bb * (nb // 2) + m
    prev = lambda m: jnp.maximum(2 * m - 1, 0)
    nxt = lambda m: jnp.minimum(2 * m + 2, nb - 1)
    smem = pl.BlockSpec(memory_space=pltpu.SMEM)
    out_lat = pl.pallas_call(
        functools.partial(_attn_latent_kernel, n_blocks=nb),
        out_shape=jax.ShapeDtypeStruct((n_tok, D_BRANCH), BF16),
        grid=(b, nb // 2),
        in_specs=[
            smem,
            pl.BlockSpec((pair, D_BRANCH), lambda bb, m: (lat2(bb, m), qcol)),
            pl.BlockSpec((CHUNK, kvw), lambda bb, m: (lat(bb, prev(m)), kcol)),
            pl.BlockSpec((pair, kvw), lambda bb, m: (lat2(bb, m), kcol)),
            pl.BlockSpec((CHUNK, kvw), lambda bb, m: (lat(bb, nxt(m)), kcol)),
            pl.BlockSpec((CHUNK, kvw), lambda bb, m: (lat(bb, prev(m)), vcol)),
            pl.BlockSpec((pair, kvw), lambda bb, m: (lat2(bb, m), vcol)),
            pl.BlockSpec((CHUNK, kvw), lambda bb, m: (lat(bb, nxt(m)), vcol)),
            pl.BlockSpec((n_ctx, kvw), lambda bb, m: (bb, kcol)),
            pl.BlockSpec((n_ctx, kvw), lambda bb, m: (bb, vcol)),
            pl.BlockSpec((3, pair, kvw), lambda bb, m: (0, m, 0)),
            pl.BlockSpec((3, CHUNK, kvw), lambda bb, m: (0, prev(m), 0)),
            pl.BlockSpec((3, CHUNK, kvw), lambda bb, m: (0, nxt(m), 0)),
        ],
        out_specs=pl.BlockSpec((pair, D_BRANCH), lambda bb, m: (lat2(bb, m), 0)),
        compiler_params=_cparams("arbitrary", "arbitrary"),
        name="attention_latent",
    )(sink, z, z, z, z, z, z, z, z, z, rope_tab, rope_tab, rope_tab)
    ncb = n_ctx // CHUNK
    return pl.pallas_call(
        _attn_ctx_kernel,
        out_shape=jax.ShapeDtypeStruct((n_tok, D_BRANCH), BF16),
        grid=(b, ncb),
        in_specs=[
            smem,
            pl.BlockSpec((CHUNK, D_BRANCH), lambda bb, n: (bb * ncb + n, qcol)),
            pl.BlockSpec((n_ctx, kvw), lambda bb, n: (bb, kcol)),
            pl.BlockSpec((n_ctx, kvw), lambda bb, n: (bb, vcol)),
            pl.BlockSpec(memory_space=pl.ANY),
        ],
        out_specs=pl.BlockSpec((CHUNK, D_BRANCH), lambda bb, n: (bb * ncb + n, 0)),
        input_output_aliases={4: 0},
        compiler_params=_cparams("arbitrary", "arbitrary"),
        name="attention_ctx",
    )(sink, z, z, z, out_lat)


def _s5_tables(a_re, a_im, log_step, b_re, b_im, c_re, c_im):
    t_len = CHUNK
    hi = lax.Precision.HIGHEST
    j = jnp.arange(t_len + 1, dtype=F32)
    lam = lax.complex(a_re.astype(F32), a_im.astype(F32))
    log_lbar = lam * jnp.exp(log_step.astype(F32))[..., None]
    lbar = jnp.exp(log_lbar)
    b = lax.complex(b_re.astype(F32), b_im.astype(F32))
    bbar = ((lbar - 1.0) / lam)[..., None] * b[None]
    cc = lax.complex(c_re.astype(F32), c_im.astype(F32))
    pw = jnp.exp(log_lbar[:, :, None, :] * j[None, None, :, None])
    pw_re, pw_im = pw.real, pw.imag

    q = cc[:, :, :, None, :] * bbar.transpose(0, 1, 3, 2)[:, :, None, :, :]
    taps = (jnp.einsum("dgcip,dgjp->dgjci", q.real, pw_re[:, :, :t_len], precision=hi)
            - jnp.einsum("dgcip,dgjp->dgjci", q.imag, pw_im[:, :, :t_len], precision=hi))
    kf, kb = taps[0], taps[1]
    kc = jnp.concatenate([kb[:, :0:-1], (kf[:, :1] + kb[:, :1]), kf[:, 1:]], axis=1)
    strip = kc.transpose(0, 3, 1, 2).reshape(S5_GROUPS, S5_GW, (2 * t_len - 1) * S5_GW)
    width = 2 * t_len * S5_GW
    strip = jnp.pad(strip, ((0, 0), (0, 0), (0, width + 128 - strip.shape[-1])))

    def lanes4(f, bk):
        return jnp.concatenate([f, bk, f, bk], axis=-1)

    bt = bbar.transpose(0, 1, 3, 2)
    p1 = lanes4(pw_re[0, :, t_len - 1::-1], pw_re[1, :, :t_len])
    p2 = lanes4(pw_im[0, :, t_len - 1::-1], pw_im[1, :, :t_len])
    x1 = jnp.concatenate([bt[0].real, bt[1].real, bt[0].imag, bt[1].imag], axis=-1)
    x2 = jnp.concatenate([-bt[0].imag, -bt[1].imag, bt[0].real, bt[1].real], axis=-1)
    q1 = lanes4(pw_re[0, :, 1:t_len + 1], pw_re[1, :, t_len:0:-1])
    q2 = lanes4(pw_im[0, :, 1:t_len + 1], pw_im[1, :, t_len:0:-1])
    y1 = jnp.concatenate([cc[0].real, cc[1].real, -cc[0].imag, -cc[1].imag], axis=-1)
    y2 = jnp.concatenate([-cc[0].imag, -cc[1].imag, -cc[0].real, -cc[1].real], axis=-1)
    rows = jnp.stack([p1, p2, q1, q2], axis=1)
    cols = jnp.stack([x1, x2, y1, y2], axis=1)

    dre = jnp.concatenate([pw_re[0, :, t_len], pw_re[1, :, t_len]], axis=-1)
    dim = jnp.concatenate([pw_im[0, :, t_len], pw_im[1, :, t_len]], axis=-1)
    decay = jnp.stack([dre, dim], axis=1)
    return strip.astype(F32), rows.astype(F32), cols.astype(F32), decay.astype(F32)


def _s5_kernel(u_ref, strip_ref, rows_ref, cols_ref, dec_ref, dskip_ref, y_ref,
               mt_ref, strips_ref, wst_ref, clt_ref, dsre_ref, dsim_ref, fre_ref, fim_ref, bre_ref, bim_ref,
               *, batch, n_ctx_steps, n_lat_steps):
    t_len = CHUNK
    width = t_len * S5_GW
    lanes = 2 * S5_STATE
    strip = strip_ref[0]
    for r in range(8):
        strips_ref[r] = strip[:, r * S5_GW:r * S5_GW + 2 * width].astype(BF16)
    for s in range(t_len):
        off = (t_len - 1 - s) * S5_GW
        r, q = (off // S5_GW) % 8, off // 128
        mt_ref[s * S5_GW:(s + 1) * S5_GW, :] = strips_ref[r, :, q * 128:q * 128 + width]
    for dst_ref, k in ((wst_ref, 0), (clt_ref, 2)):
        col_a, col_b = cols_ref[0, k], cols_ref[0, k + 1]
        for s in range(t_len):
            blk = col_a * rows_ref[0, k, s:s + 1, :] + col_b * rows_ref[0, k + 1, s:s + 1, :]
            dst_ref[s * S5_GW:(s + 1) * S5_GW, :] = blk.astype(BF16)
    u = u_ref[0]
    ds = jnp.dot(u, wst_ref[...], preferred_element_type=F32)
    dsre_ref[...] = ds[:, :lanes]
    dsim_ref[...] = ds[:, lanes:]

    dre = jnp.broadcast_to(dec_ref[0, 0:1, :], (batch, lanes))
    dim = jnp.broadcast_to(dec_ref[0, 1:2, :], (batch, lanes))
    is_fwd = lax.broadcasted_iota(jnp.int32, (batch, lanes), 1) < S5_STATE

    def run(first_row, n_steps, carry):
        def step(it, carry):
            sre, sim = carry
            rows_f = pl.ds(first_row + it, batch, stride=n_steps)
            rows_b = pl.ds(first_row + n_steps - 1 - it, batch, stride=n_steps)
            fre_ref[rows_f, :] = sre
            fim_ref[rows_f, :] = sim
            bre_ref[rows_b, :] = sre
            bim_ref[rows_b, :] = sim
            add_re = jnp.where(is_fwd, dsre_ref[rows_f, :], dsre_ref[rows_b, :])
            add_im = jnp.where(is_fwd, dsim_ref[rows_f, :], dsim_ref[rows_b, :])
            return sre * dre - sim * dim + add_re, sre * dim + sim * dre + add_im

        return lax.fori_loop(0, n_steps, step, carry)

    zero = jnp.zeros((batch, lanes), F32)
    carry = run(0, n_ctx_steps, (zero, zero))
    run(batch * n_ctx_steps, n_lat_steps, carry)

    n_rows = dsre_ref.shape[0]
    fwd_rows = lax.broadcasted_iota(jnp.int32, (n_rows, lanes), 1) < S5_STATE
    sin = jnp.concatenate([jnp.where(fwd_rows, fre_ref[...], bre_ref[...]),
                           jnp.where(fwd_rows, fim_ref[...], bim_ref[...])], axis=1).astype(BF16)
    y = jnp.dot(u, mt_ref[...], preferred_element_type=F32)
    y = y + lax.dot_general(sin, clt_ref[...], (((1,), (1,)), ((), ())), preferred_element_type=F32)
    y_ref[0] = _gelu_tanh(y + dskip_ref[0] * u.astype(F32)).astype(y_ref.dtype)


def _s5_mix(ud, tables, d_skip, lay):
    b, s, n_ctx = lay["b"], lay["s"], lay["n_ctx"]
    strip, rows, cols, decay = tables
    width = CHUNK * S5_GW
    ncs, nls = n_ctx // CHUNK, s // CHUNK
    m = b * (ncs + nls)
    ug = ud.reshape(m, CHUNK, S5_GROUPS, S5_GW).transpose(2, 0, 1, 3).reshape(S5_GROUPS, m, width)
    dskip = jnp.broadcast_to(d_skip.astype(F32).reshape(S5_GROUPS, 1, 1, S5_GW),
                             (S5_GROUPS, 1, CHUNK, S5_GW)).reshape(S5_GROUPS, 1, width)
    state_rows = pltpu.VMEM((m, 2 * S5_STATE), F32)
    y = pl.pallas_call(
        functools.partial(_s5_kernel, batch=b, n_ctx_steps=ncs, n_lat_steps=nls),
        out_shape=jax.ShapeDtypeStruct((S5_GROUPS, m, width), BF16),
        grid=(S5_GROUPS,),
        in_specs=[
            pl.BlockSpec((1, m, width), lambda g: (g, 0, 0)),
            pl.BlockSpec((1, S5_GW, 2 * width + 128), lambda g: (g, 0, 0)),
            pl.BlockSpec((1, 4, CHUNK, 4 * S5_STATE), lambda g: (g, 0, 0, 0)),
            pl.BlockSpec((1, 4, S5_GW, 4 * S5_STATE), lambda g: (g, 0, 0, 0)),
            pl.BlockSpec((1, 2, 2 * S5_STATE), lambda g: (g, 0, 0)),
            pl.BlockSpec((1, 1, width), lambda g: (g, 0, 0)),
        ],
        out_specs=pl.BlockSpec((1, m, width), lambda g: (g, 0, 0)),
        scratch_shapes=[
            pltpu.VMEM((width, width), BF16),
            pltpu.VMEM((8, S5_GW, 2 * width), BF16),
            pltpu.VMEM((width, 4 * S5_STATE), BF16),
            pltpu.VMEM((width, 4 * S5_STATE), BF16),
            state_rows, state_rows, state_rows, state_rows, state_rows, state_rows,
        ],
        compiler_params=_cparams("arbitrary"),
        name="s5_scan",
    )(ug, strip, rows, cols, decay, dskip)
    return y.reshape(S5_GROUPS, m, CHUNK, S5_GW).transpose(1, 2, 0, 3).reshape(m * CHUNK, D_BRANCH)


def _s5_readout_kernel(y_ref, w_ref, o_ref):
    r = jnp.dot(y_ref[...], w_ref[...], preferred_element_type=F32)
    o_ref[...] = (r[:, :D_BRANCH] * _sigmoid(r[:, D_BRANCH:])).astype(o_ref.dtype)


def _s5_readout(y, w_glu, layer, lay):
    n_tok = y.shape[0]
    rows = lay["tm"]
    return pl.pallas_call(
        _s5_readout_kernel,
        out_shape=jax.ShapeDtypeStruct((n_tok, D_BRANCH), BF16),
        grid=(n_tok // rows,),
        in_specs=[
            pl.BlockSpec((rows, D_BRANCH), lambda i: (i, 0)),
            pl.BlockSpec((None, D_BRANCH, 2 * D_BRANCH), lambda i: (layer, 0, 0)),
        ],
        out_specs=pl.BlockSpec((rows, D_BRANCH), lambda i: (i, 0)),
        compiler_params=_cparams("arbitrary"),
        name="s5_readout",
    )(y, w_glu)


def _merge_kernel(h_ref, ba_ref, bb_ref, bc_ref, bd_ref, wg_ref, bg_ref, wb_ref, o_ref):
    h = h_ref[...]
    acc = None
    for k, br_ref in enumerate((ba_ref, bb_ref, bc_ref, bd_ref)):
        gate = _sigmoid(jnp.dot(h, wg_ref[k], preferred_element_type=F32) + bg_ref[k])
        term = gate * jnp.dot(br_ref[...], wb_ref[k], preferred_element_type=F32)
        acc = term if acc is None else acc + term
    o_ref[...] = acc.astype(o_ref.dtype)


def _merge(h, branches, w_gate, b_gate, w_branch, layer, first, lay):
    n_tok, d = h.shape
    tm, tn = lay["tm"], 512
    br_spec = pl.BlockSpec((tm, D_BRANCH), lambda i, j: (i + first, 0))
    return pl.pallas_call(
        _merge_kernel,
        out_shape=jax.ShapeDtypeStruct((n_tok, d), BF16),
        grid=(n_tok // tm - first, d // tn),
        in_specs=[
            pl.BlockSpec((tm, d), lambda i, j: (i + first, 0)),
            br_spec, br_spec, br_spec, br_spec,
            pl.BlockSpec((None, N_BRANCH, d, tn), lambda i, j: (layer, 0, 0, j)),
            pl.BlockSpec((N_BRANCH, 1, tn), lambda i, j: (0, 0, j)),
            pl.BlockSpec((None, N_BRANCH, D_BRANCH, tn), lambda i, j: (layer, 0, 0, j)),
        ],
        out_specs=pl.BlockSpec((tm, tn), lambda i, j: (i + first, j)),
        compiler_params=_cparams("arbitrary", "arbitrary"),
        name="branch_merge",
    )(h, *branches, w_gate, b_gate.reshape(N_BRANCH, 1, d), w_branch)


def _outproj_kernel(m_ref, w_ref, x_ref, gate_ref, o_ref):
    o_ref[...] = x_ref[...] + gate_ref[0] * jnp.dot(m_ref[...], w_ref[...], preferred_element_type=F32)


def _out_projection(merged, w_out, xall, mod3, layer, first, lay):
    n_tok, d = xall.shape
    split = 2
    tm = lay["tm"] // split
    mod_row = lay["mod_row"]
    skip = first * split
    return pl.pallas_call(
        _outproj_kernel,
        out_shape=jax.ShapeDtypeStruct((n_tok, d), F32),
        grid=(n_tok // tm - skip,),
        in_specs=[
            pl.BlockSpec((tm, d), lambda i: (i + skip, 0)),
            pl.BlockSpec((None, d, d), lambda i: (layer, 0, 0)),
            pl.BlockSpec((tm, d), lambda i: (i + skip, 0)),
            pl.BlockSpec((1, 1, d), lambda i: (mod_row((i + skip) // split, 2), 0, 0)),
        ],
        out_specs=pl.BlockSpec((tm, d), lambda i: (i + skip, 0)),
        compiler_params=_cparams("arbitrary"),
        name="out_projection",
    )(merged, w_out, xall, mod3)


def _ffn_kernel(x_ref, g_ref, shift_ref, scale_ref, gate_ref, w1_ref, w2_ref, o_ref, h_ref):
    j = pl.program_id(1)

    @pl.when(j == 0)
    def _():
        _modulated_rms_norm(x_ref, g_ref, shift_ref, scale_ref, h_ref)
        o_ref[...] = jnp.zeros_like(o_ref)

    a = jnp.maximum(jnp.dot(h_ref[...], w1_ref[...], preferred_element_type=F32), 0.0)
    o_ref[...] += jnp.dot((a * a).astype(BF16), w2_ref[...], preferred_element_type=F32)

    @pl.when(j == pl.num_programs(1) - 1)
    def _():
        o_ref[...] = x_ref[...] + gate_ref[0] * o_ref[...]


def _ffn(xall, norm_g, mod3, w1, w2, layer, first, lay):
    n_tok, d = xall.shape
    tm, tf = lay["tm"], 512
    mod_row = lay["mod_row"]
    mspec = lambda which: pl.BlockSpec((1, 1, d), lambda i, j: (mod_row(i + first, which), 0, 0))
    return pl.pallas_call(
        _ffn_kernel,
        out_shape=jax.ShapeDtypeStruct((n_tok, d), F32),
        grid=(n_tok // tm - first, D_FF // tf),
        in_specs=[
            pl.BlockSpec((tm, d), lambda i, j: (i + first, 0)),
            pl.BlockSpec((1, d), lambda i, j: (0, 0)),
            mspec(3), mspec(4), mspec(5),
            pl.BlockSpec((None, d, tf), lambda i, j: (layer, 0, j)),
            pl.BlockSpec((None, tf, d), lambda i, j: (layer, j, 0)),
        ],
        out_specs=pl.BlockSpec((tm, d), lambda i, j: (i + first, 0)),
        scratch_shapes=[pltpu.VMEM((tm, d), BF16)],
        compiler_params=_cparams("arbitrary", "arbitrary"),
        name="ffn",
    )(xall, norm_g.reshape(1, d), mod3, mod3, mod3, w1, w2)


def _final_norm_kernel(x_ref, g_ref, o_ref):
    x = x_ref[...]
    o_ref[...] = x * lax.rsqrt(jnp.mean(x * x, axis=-1, keepdims=True) + EPS) * g_ref[...]


def _final_norm(xall, g, lay):
    d = xall.shape[1]
    tm = lay["tm"]
    first = lay["nc"] // tm
    n_lat = lay["b"] * lay["s"]
    return pl.pallas_call(
        _final_norm_kernel,
        out_shape=jax.ShapeDtypeStruct((n_lat, d), F32),
        grid=(n_lat // tm,),
        in_specs=[pl.BlockSpec((tm, d), lambda i: (i + first, 0)), pl.BlockSpec((1, d), lambda i: (0, 0))],
        out_specs=pl.BlockSpec((tm, d), lambda i: (i, 0)),
        compiler_params=_cparams("arbitrary"),
        name="final_norm",
    )(xall, g.reshape(1, d))


def _reorder_in_columns(w_in):
    q_end = 5 * D_BRANCH
    kv = 2 * N_KV_HEADS * HEAD_DIM
    pad = jnp.zeros(w_in.shape[:-1] + (IN_COLS_PAD - IN_COLS,), w_in.dtype)
    return jnp.concatenate([w_in[..., :q_end], w_in[..., q_end + kv:], w_in[..., q_end:q_end + kv], pad], axis=-1)


def kernel(x, c, ctx, c_ctx, w_mod, b_mod, norm1_g, norm2_g, w_in, gmlp_ln_g, gmlp_ln_b, gmlp_ws, gmlp_bs, conv_w, conv_b, conv_ln_g, conv_ln_b, attn_sink, s5_a_re, s5_a_im, s5_log_step, s5_b_re, s5_b_im, s5_c_re, s5_c_im, s5_d, s5_w_glu, w_branch, w_gate, b_gate, w_out, w_ff1, w_ff2, final_g):
    b, s, d = x.shape
    n_ctx = ctx.shape[1]
    depth = w_mod.shape[0]
    assert d == D_MODEL and s % GRID_W == 0
    nc = b * n_ctx
    tm = _row_tile(nc, s)
    tseq = min(256, tm)
    nct, lpt = nc // tm, s // tm

    def mod_row(i, which):
        bidx = jnp.where(i < nct, b, (i - nct) // lpt)
        return bidx * N_MOD + which

    lay = dict(b=b, s=s, n_ctx=n_ctx, nc=nc, tm=tm, tseq=tseq, mod_row=mod_row)

    cond = jnp.concatenate([c, c_ctx[None, :]], axis=0)
    cond = jnp.pad(cond, ((0, -cond.shape[0] % 8), (0, 0)))
    mod_all = _modulation(cond, w_mod, b_mod)
    rope_tab = _rope_tables(s)
    xall = jnp.concatenate([ctx.reshape(nc, d), x.reshape(b * s, d)], axis=0)

    w_in_b = _reorder_in_columns(w_in).astype(BF16)
    w_gate_b, w_branch_b, w_out_b = w_gate.astype(BF16), w_branch.astype(BF16), w_out.astype(BF16)
    w_ff1_b, w_ff2_b, w_glu_b = w_ff1.astype(BF16), w_ff2.astype(BF16), s5_w_glu.astype(BF16)

    for l in range(depth):
        first = nct if l == depth - 1 else 0
        mod3 = mod_all[l, :b + 1].reshape((b + 1) * N_MOD, 1, d)
        h, z, ud = _in_projection(xall, norm1_g[l], mod3, w_in_b, l, lay)
        br_a = _gmlp(z, gmlp_ln_g[l], gmlp_ln_b[l], gmlp_ws[l], gmlp_bs[l], lay)
        br_b = _conformer_conv(z, conv_w[l], conv_b[l], conv_ln_g[l], conv_ln_b[l], lay)
        br_c = _attention(z, attn_sink[l], rope_tab, lay)
        tables = _s5_tables(s5_a_re[l], s5_a_im[l], s5_log_step[l], s5_b_re[l], s5_b_im[l], s5_c_re[l], s5_c_im[l])
        y = _s5_mix(ud, tables, s5_d[l], lay)
        br_d = _s5_readout(y, w_glu_b, l, lay)
        merged = _merge(h, (br_a, br_b, br_c, br_d), w_gate_b, b_gate[l], w_branch_b, l, first, lay)
        xall = _out_projection(merged, w_out_b, xall, mod3, l, first, lay)
        xall = _ffn(xall, norm2_g[l], mod3, w_ff1_b, w_ff2_b, l, first, lay)

    return _final_norm(xall, final_g, lay).reshape(b, s, d)
```

```python
import functools
import math

import jax
import jax.numpy as jnp
from jax import lax
from jax.experimental import pallas as pl
from jax.experimental.pallas import tpu as pltpu

D_MODEL = 2048
N_BRANCH = 4
D_BRANCH = D_MODEL // N_BRANCH
CHUNK = 128
GMLP_GROUPS = 4
CONV_W = 31
CONV_HALO = 16
HEAD_DIM = 64
N_Q_HEADS = D_BRANCH // HEAD_DIM
N_KV_HEADS = 2
Q_PER_KV = N_Q_HEADS // N_KV_HEADS
WINDOW = 128
ATTN_Q_BLOCKS = 4
ROPE_BASE = 10000.0
GRID_W = 64
S5_GW = 16
S5_GROUPS = D_BRANCH // S5_GW
S5_STATE = 64
D_FF = 4 * D_MODEL
N_MOD = 6
EPS = 1e-6
NEG_INF = -1e30

COL_A = 0
COL_B = 2 * D_BRANCH
COL_Q = 4 * D_BRANCH
COL_D = 5 * D_BRANCH
COL_K = 6 * D_BRANCH
COL_V = COL_K + N_KV_HEADS * HEAD_DIM
IN_COLS = COL_V + N_KV_HEADS * HEAD_DIM
IN_COLS_PAD = -(-IN_COLS // D_BRANCH) * D_BRANCH

VMEM_LIMIT_V7X = 56 * 1024 * 1024

F32 = jnp.float32
BF16 = jnp.bfloat16


def _cparams(*sem):
    return pltpu.CompilerParams(dimension_semantics=sem, vmem_limit_bytes=VMEM_LIMIT_V7X)


def _gelu_tanh(x):
    return 0.5 * x * (1.0 + jnp.tanh(math.sqrt(2.0 / math.pi) * (x + 0.044715 * (x * x * x))))


def _sigmoid(x):
    return 1.0 / (1.0 + jnp.exp(-x))


def _layer_norm_rows(v, g, b):
    mu = jnp.mean(v, axis=-1, keepdims=True)
    vc = v - mu
    var = jnp.mean(vc * vc, axis=-1, keepdims=True)
    return vc * lax.rsqrt(var + EPS) * g + b


def _row_tile(nc, s):
    for t in (1024, 512, 256, 128):
        if nc % t == 0 and s % t == 0:
            return t
    raise ValueError("context and sequence lengths must be multiples of 128")


def _mod_kernel(c_ref, w_ref, b_ref, o_ref):
    c = c_ref[...]
    c = (c * _sigmoid(c)).astype(BF16)
    o_ref[0] = jnp.dot(c, w_ref[0].astype(BF16), preferred_element_type=F32) + b_ref[0]


def _modulation(cond, w_mod, b_mod):
    depth, d, n = w_mod.shape
    rows = cond.shape[0]
    tn = 1024
    return pl.pallas_call(
        _mod_kernel,
        out_shape=jax.ShapeDtypeStruct((depth, rows, n), F32),
        grid=(depth, n // tn),
        in_specs=[
            pl.BlockSpec((rows, d), lambda l, j: (0, 0)),
            pl.BlockSpec((1, d, tn), lambda l, j: (l, 0, j)),
            pl.BlockSpec((1, 1, tn), lambda l, j: (l, 0, j)),
        ],
        out_specs=pl.BlockSpec((1, rows, tn), lambda l, j: (l, 0, j)),
        compiler_params=_cparams("arbitrary", "arbitrary"),
        name="modulation",
    )(cond, w_mod, b_mod.reshape(depth, 1, n))


NORM_ROWS = 128


def _modulated_rms_norm(x_ref, g_ref, shift_ref, scale_ref, h_ref):
    gain = g_ref[...] * (1.0 + scale_ref[0])
    shift = shift_ref[0]

    def body(r, carry):
        rows = pl.ds(pl.multiple_of(r * NORM_ROWS, NORM_ROWS), NORM_ROWS)
        x = x_ref[rows, :]
        inv = lax.rsqrt(jnp.mean(x * x, axis=-1, keepdims=True) + EPS)
        h_ref[rows, :] = (x * inv * gain + shift).astype(h_ref.dtype)
        return carry

    lax.fori_loop(0, x_ref.shape[0] // NORM_ROWS, body, 0)


def _inproj_kernel(x_ref, g_ref, shift_ref, scale_ref, w_ref, h_ref, z_ref, ud_ref):
    _modulated_rms_norm(x_ref, g_ref, shift_ref, scale_ref, h_ref)
    h = h_ref[...]
    for c0 in range(0, z_ref.shape[1], D_BRANCH):
        z = jnp.dot(h, w_ref[:, c0:c0 + D_BRANCH], preferred_element_type=F32).astype(z_ref.dtype)
        z_ref[:, c0:c0 + D_BRANCH] = z
        if c0 == COL_D:
            ud_ref[...] = z


def _in_projection(xall, norm_g, mod3, w_in, layer, lay):
    n_tok, d = xall.shape
    split = 2
    tm = lay["tm"] // split
    mod_row = lay["mod_row"]
    return pl.pallas_call(
        _inproj_kernel,
        out_shape=(jax.ShapeDtypeStruct((n_tok, d), BF16), jax.ShapeDtypeStruct((n_tok, IN_COLS_PAD), BF16),
                   jax.ShapeDtypeStruct((n_tok, D_BRANCH), BF16)),
        grid=(n_tok // tm,),
        in_specs=[
            pl.BlockSpec((tm, d), lambda i: (i, 0)),
            pl.BlockSpec((1, d), lambda i: (0, 0)),
            pl.BlockSpec((1, 1, d), lambda i: (mod_row(i // split, 0), 0, 0)),
            pl.BlockSpec((1, 1, d), lambda i: (mod_row(i // split, 1), 0, 0)),
            pl.BlockSpec((None, d, IN_COLS_PAD), lambda i: (layer, 0, 0)),
        ],
        out_specs=(pl.BlockSpec((tm, d), lambda i: (i, 0)), pl.BlockSpec((tm, IN_COLS_PAD), lambda i: (i, 0)),
                   pl.BlockSpec((tm, D_BRANCH), lambda i: (i, 0))),
        compiler_params=_cparams("arbitrary"),
        name="in_projection",
    )(xall, norm_g.reshape(1, d), mod3, mod3, w_in)


def _gmlp_kernel(za_ref, lng_ref, lnb_ref, ws_ref, bs_ref, o_ref):
    rows = za_ref.shape[0]
    gw = D_BRANCH // GMLP_GROUPS
    a = _gelu_tanh(za_ref[...].astype(F32))
    u = a[:, :D_BRANCH]
    v = _layer_norm_rows(a[:, D_BRANCH:], lng_ref[...], lnb_ref[...]).astype(BF16)
    for c in range(rows // CHUNK):
        r0 = c * CHUNK
        for g in range(GMLP_GROUPS):
            c0 = g * gw
            mixed = jnp.dot(ws_ref[g], v[r0:r0 + CHUNK, c0:c0 + gw], preferred_element_type=F32) + bs_ref[g]
            o_ref[r0:r0 + CHUNK, c0:c0 + gw] = (u[r0:r0 + CHUNK, c0:c0 + gw] * mixed).astype(o_ref.dtype)


def _gmlp(z, ln_g, ln_b, ws, bs, lay):
    n_tok = z.shape[0]
    rows = lay["tm"]
    gw = D_BRANCH // GMLP_GROUPS
    bs_full = jnp.broadcast_to(bs[:, :, None], (GMLP_GROUPS, CHUNK, gw)).astype(F32)
    return pl.pallas_call(
        _gmlp_kernel,
        out_shape=jax.ShapeDtypeStruct((n_tok, D_BRANCH), BF16),
        grid=(n_tok // rows,),
        in_specs=[
            pl.BlockSpec((rows, 2 * D_BRANCH), lambda i: (i, COL_A // (2 * D_BRANCH))),
            pl.BlockSpec((1, D_BRANCH), lambda i: (0, 0)),
            pl.BlockSpec((1, D_BRANCH), lambda i: (0, 0)),
            pl.BlockSpec((GMLP_GROUPS, CHUNK, CHUNK), lambda i: (0, 0, 0)),
            pl.BlockSpec((GMLP_GROUPS, CHUNK, gw), lambda i: (0, 0, 0)),
        ],
        out_specs=pl.BlockSpec((rows, D_BRANCH), lambda i: (i, 0)),
        compiler_params=_cparams("arbitrary"),
        name="gmlp_mix",
    )(z, ln_g.reshape(1, -1), ln_b.reshape(1, -1), ws.astype(BF16), bs_full)


def _conv_kernel(zm_ref, zp_ref, zn_ref, w_ref, cb_ref, lng_ref, lnb_ref, o_ref, ybuf, *, seg_tiles):
    rows = zm_ref.shape[0]
    i = pl.program_id(0)
    nct, cpt, lpt = seg_tiles
    pos = jnp.where(i < nct, i % cpt, (i - nct) % lpt)
    seg = jnp.where(i < nct, cpt, lpt)
    keep_prev = (pos != 0).astype(F32)
    keep_next = (pos != seg - 1).astype(F32)

    def glu(ref):
        z = ref[...].astype(F32)
        return z[:, :D_BRANCH] * _sigmoid(z[:, D_BRANCH:])

    ybuf[0:CONV_HALO, :] = glu(zp_ref) * keep_prev
    ybuf[CONV_HALO:CONV_HALO + rows, :] = glu(zm_ref)
    ybuf[CONV_HALO + rows:, :] = glu(zn_ref) * keep_next

    sub, tile = 32, 8
    off = CONV_HALO - CONV_W // 2
    for r in range(rows // sub):
        acc = jnp.zeros((sub, D_BRANCH), F32)
        for res in range(tile):
            part = None
            for k in range(CONV_W):
                if (off + k) % tile != res:
                    continue
                start = r * sub + off + k - res
                term = ybuf[start:start + sub + tile, :] * w_ref[k:k + 1, :]
                part = term if part is None else part + term
            if part is not None:
                acc = acc + part[res:res + sub, :]
        y = _layer_norm_rows(acc + cb_ref[...], lng_ref[...], lnb_ref[...])
        o_ref[r * sub:(r + 1) * sub, :] = (y * _sigmoid(y)).astype(o_ref.dtype)


def _conformer_conv(z, w_dw, b_dw, ln_g, ln_b, lay):
    n_tok = z.shape[0]
    rows = lay["tseq"]
    hb = rows // CONV_HALO
    n_halo_blocks = n_tok // CONV_HALO
    colb = COL_B // (2 * D_BRANCH)
    seg_tiles = (lay["nc"] // rows, lay["n_ctx"] // rows, lay["s"] // rows)
    vec = lambda a: a.reshape(1, -1)
    return pl.pallas_call(
        functools.partial(_conv_kernel, seg_tiles=seg_tiles),
        out_shape=jax.ShapeDtypeStruct((n_tok, D_BRANCH), BF16),
        grid=(n_tok // rows,),
        in_specs=[
            pl.BlockSpec((rows, 2 * D_BRANCH), lambda i: (i, colb)),
            pl.BlockSpec((CONV_HALO, 2 * D_BRANCH), lambda i: (jnp.maximum(i * hb - 1, 0), colb)),
            pl.BlockSpec((CONV_HALO, 2 * D_BRANCH), lambda i: (jnp.minimum((i + 1) * hb, n_halo_blocks - 1), colb)),
            pl.BlockSpec((CONV_W, D_BRANCH), lambda i: (0, 0)),
            pl.BlockSpec((1, D_BRANCH), lambda i: (0, 0)),
            pl.BlockSpec((1, D_BRANCH), lambda i: (0, 0)),
            pl.BlockSpec((1, D_BRANCH), lambda i: (0, 0)),
        ],
        out_specs=pl.BlockSpec((rows, D_BRANCH), lambda i: (i, 0)),
        scratch_shapes=[pltpu.VMEM((rows + 2 * CONV_HALO, D_BRANCH), F32)],
        compiler_params=_cparams("arbitrary"),
        name="conformer_conv",
    )(z, z, z, w_dw, vec(b_dw), vec(ln_g), vec(ln_b))


def _rope(x, cos, sin_a, sin_b):
    reps = x.shape[1] // cos.shape[1]
    if reps > 1:
        cos, sin_a, sin_b = (jnp.concatenate([t] * reps, axis=1) for t in (cos, sin_a, sin_b))
    n = x.shape[1]
    half = HEAD_DIM // 4
    return x * cos + pltpu.roll(x, n - half, 1) * sin_a + pltpu.roll(x, half, 1) * sin_b


def _attn_core(q, keys, vals, bias, sink_ref, o_ref, r0=0):
    rows = q.shape[0]
    rid = lax.broadcasted_iota(jnp.int32, (Q_PER_KV * rows, 1), 0)
    for h in range(N_KV_HEADS):
        kh = keys[:, h * HEAD_DIM:(h + 1) * HEAD_DIM]
        vh = vals[:, h * HEAD_DIM:(h + 1) * HEAD_DIM]
        heads = [q[:, (h * Q_PER_KV + g) * HEAD_DIM:(h * Q_PER_KV + g + 1) * HEAD_DIM] for g in range(Q_PER_KV)]
        qh = jnp.concatenate(heads, axis=0).astype(BF16)
        s = lax.dot_general(qh, kh, (((1,), (1,)), ((), ())), preferred_element_type=F32)
        if bias is not None:
            s = s + jnp.concatenate([bias] * Q_PER_KV, axis=0)
        sink = jnp.full((Q_PER_KV * rows, 1), sink_ref[h * Q_PER_KV], F32)
        for g in range(1, Q_PER_KV):
            sink = jnp.where(rid >= g * rows, sink_ref[h * Q_PER_KV + g], sink)
        m = jnp.maximum(jnp.max(s, axis=-1, keepdims=True), sink)
        e = jnp.exp(s - m)
        denom = jnp.sum(e, axis=-1, keepdims=True) + jnp.exp(sink - m)
        o = jnp.dot(e.astype(BF16), vh, preferred_element_type=F32) * (1.0 / denom)
        for g in range(Q_PER_KV):
            c0 = (h * Q_PER_KV + g) * HEAD_DIM
            o_ref[r0:r0 + rows, c0:c0 + HEAD_DIM] = o[g * rows:(g + 1) * rows, :].astype(o_ref.dtype)


def _attn_latent_kernel(sink_ref, q_ref, kp_ref, kc_ref, kn_ref, vp_ref, vc_ref, vn_ref, kx_ref, vx_ref,
                        cq_ref, cp_ref, cn_ref, o_ref, *, n_blocks):
    m = pl.program_id(1)
    rows = CHUNK
    scale = HEAD_DIM ** -0.5

    def tables(ref):
        return ref[0], ref[1], ref[2]

    def split(x):
        return [x[i * rows:(i + 1) * rows] for i in range(ATTN_Q_BLOCKS)]

    f32 = lambda ref: ref[...].astype(F32)
    q = _rope(f32(q_ref), *tables(cq_ref)) * scale
    k_blocks = ([_rope(f32(kp_ref), *tables(cp_ref)).astype(BF16)]
                + split(_rope(f32(kc_ref), *tables(cq_ref)).astype(BF16))
                + [_rope(f32(kn_ref), *tables(cn_ref)).astype(BF16)])
    v_blocks = [vp_ref[...].astype(BF16)] + split(vc_ref[...].astype(BF16)) + [vn_ref[...].astype(BF16)]
    kx, vx = kx_ref[...].astype(BF16), vx_ref[...].astype(BF16)
    nk = 3 * rows + kx.shape[0]
    qi = lax.broadcasted_iota(jnp.int32, (rows, nk), 0)
    kj = lax.broadcasted_iota(jnp.int32, (rows, nk), 1)
    rel = kj - qi
    for half in range(ATTN_Q_BLOCKS):
        n = ATTN_Q_BLOCKS * m + half
        keys = jnp.concatenate(k_blocks[half:half + 3] + [kx], axis=0)
        vals = jnp.concatenate(v_blocks[half:half + 3] + [vx], axis=0)
        lo = jnp.where(n > 0, 0, rows)
        hi = jnp.where(n < n_blocks - 1, 3 * rows, 2 * rows)
        ok = (rel >= 0) & (rel <= 2 * WINDOW) & (kj >= lo) & (kj < hi)
        ok = ok | (kj >= 3 * rows)
        bias = jnp.where(ok, 0.0, NEG_INF).astype(F32)
        _attn_core(q[half * rows:(half + 1) * rows], keys, vals, bias, sink_ref, o_ref, r0=half * rows)


def _attn_ctx_kernel(sink_ref, q_ref, kx_ref, vx_ref, latent_out_ref, o_ref):
    del latent_out_ref
    q = q_ref[...].astype(F32) * (HEAD_DIM ** -0.5)
    _attn_core(q, kx_ref[...].astype(BF16), vx_ref[...].astype(BF16), None, sink_ref, o_ref)


def _rope_tables(s):
    t = jnp.arange(s)
    row = (t // GRID_W).astype(F32)[:, None]
    col = (t % GRID_W).astype(F32)[:, None]
    half = HEAD_DIM // 2
    inv = ROPE_BASE ** (-jnp.arange(0, half, 2, dtype=F32) / half)
    lane = jnp.arange(2 * HEAD_DIM)
    d = lane % HEAD_DIM
    freq = inv[(d % half) % (half // 2)][None, :]
    ang = jnp.where((d < half)[None, :], row * freq, col * freq)
    first = ((d % half) < half // 2)[None, :]
    sin = jnp.sin(ang)
    return jnp.stack([jnp.cos(ang), jnp.where(first, -sin, 0.0), jnp.where(first, 0.0, sin)]).astype(F32)


def _attention(z, sink, rope_tab, lay):
    n_tok = z.shape[0]
    b, s, n_ctx, nc = lay["b"], lay["s"], lay["n_ctx"], lay["nc"]
    nb = s // CHUNK
    cb0 = nc // CHUNK
    kvw = N_KV_HEADS * HEAD_DIM
    qcol, kcol, vcol = COL_Q // D_BRANCH, COL_K // kvw, COL_V // kvw
    nq = ATTN_Q_BLOCKS
    assert nb % nq == 0 and cb0 % nq == 0
    pair = nq * CHUNK
    lat = lambda bb, n: cb0 + bb * nb + n
    lat2 = lambda bb, m: cb0 // nq + bb * (nb // nq) + m
    prev = lambda m: jnp.maximum(nq * m - 1, 0)
    nxt = lambda m: jnp.minimum(nq * m + nq, nb - 1)
    smem = pl.BlockSpec(memory_space=pltpu.SMEM)
    out_lat = pl.pallas_call(
        functools.partial(_attn_latent_kernel, n_blocks=nb),
        out_shape=jax.ShapeDtypeStruct((n_tok, D_BRANCH), BF16),
        grid=(b, nb // nq),
        in_specs=[
            smem,
            pl.BlockSpec((pair, D_BRANCH), lambda bb, m: (lat2(bb, m), qcol)),
            pl.BlockSpec((CHUNK, kvw), lambda bb, m: (lat(bb, prev(m)), kcol)),
            pl.BlockSpec((pair, kvw), lambda bb, m: (lat2(bb, m), kcol)),
            pl.BlockSpec((CHUNK, kvw), lambda bb, m: (lat(bb, nxt(m)), kcol)),
            pl.BlockSpec((CHUNK, kvw), lambda bb, m: (lat(bb, prev(m)), vcol)),
            pl.BlockSpec((pair, kvw), lambda bb, m: (lat2(bb, m), vcol)),
            pl.BlockSpec((CHUNK, kvw), lambda bb, m: (lat(bb, nxt(m)), vcol)),
            pl.BlockSpec((n_ctx, kvw), lambda bb, m: (bb, kcol)),
            pl.BlockSpec((n_ctx, kvw), lambda bb, m: (bb, vcol)),
            pl.BlockSpec((3, pair, kvw), lambda bb, m: (0, m, 0)),
            pl.BlockSpec((3, CHUNK, kvw), lambda bb, m: (0, prev(m), 0)),
            pl.BlockSpec((3, CHUNK, kvw), lambda bb, m: (0, nxt(m), 0)),
        ],
        out_specs=pl.BlockSpec((pair, D_BRANCH), lambda bb, m: (lat2(bb, m), 0)),
        compiler_params=_cparams("arbitrary", "arbitrary"),
        name="attention_latent",
    )(sink, z, z, z, z, z, z, z, z, z, rope_tab, rope_tab, rope_tab)
    ncb = n_ctx // CHUNK
    return pl.pallas_call(
        _attn_ctx_kernel,
        out_shape=jax.ShapeDtypeStruct((n_tok, D_BRANCH), BF16),
        grid=(b, ncb),
        in_specs=[
            smem,
            pl.BlockSpec((CHUNK, D_BRANCH), lambda bb, n: (bb * ncb + n, qcol)),
            pl.BlockSpec((n_ctx, kvw), lambda bb, n: (bb, kcol)),
            pl.BlockSpec((n_ctx, kvw), lambda bb, n: (bb, vcol)),
            pl.BlockSpec(memory_space=pl.ANY),
        ],
        out_specs=pl.BlockSpec((CHUNK, D_BRANCH), lambda bb, n: (bb * ncb + n, 0)),
        input_output_aliases={4: 0},
        compiler_params=_cparams("arbitrary", "arbitrary"),
        name="attention_ctx",
    )(sink, z, z, z, out_lat)


def _s5_tables(a_re, a_im, log_step, b_re, b_im, c_re, c_im):
    t_len = CHUNK
    hi = lax.Precision.HIGHEST
    j = jnp.arange(t_len + 1, dtype=F32)
    lam = lax.complex(a_re.astype(F32), a_im.astype(F32))
    log_lbar = lam * jnp.exp(log_step.astype(F32))[..., None]
    lbar = jnp.exp(log_lbar)
    b = lax.complex(b_re.astype(F32), b_im.astype(F32))
    bbar = ((lbar - 1.0) / lam)[..., None] * b[None]
    cc = lax.complex(c_re.astype(F32), c_im.astype(F32))
    pw = jnp.exp(log_lbar[:, :, None, :] * j[None, None, :, None])
    pw_re, pw_im = pw.real, pw.imag

    q = cc[:, :, :, None, :] * bbar.transpose(0, 1, 3, 2)[:, :, None, :, :]
    taps = (jnp.einsum("dgcip,dgjp->dgjci", q.real, pw_re[:, :, :t_len], precision=hi)
            - jnp.einsum("dgcip,dgjp->dgjci", q.imag, pw_im[:, :, :t_len], precision=hi))
    kf, kb = taps[0], taps[1]
    kc = jnp.concatenate([kb[:, :0:-1], (kf[:, :1] + kb[:, :1]), kf[:, 1:]], axis=1)
    strip = kc.transpose(0, 3, 1, 2).reshape(S5_GROUPS, S5_GW, (2 * t_len - 1) * S5_GW)
    width = 2 * t_len * S5_GW
    strip = jnp.pad(strip, ((0, 0), (0, 0), (0, width + 128 - strip.shape[-1])))

    def lanes4(f, bk):
        return jnp.concatenate([f, bk, f, bk], axis=-1)

    bt = bbar.transpose(0, 1, 3, 2)
    p1 = lanes4(pw_re[0, :, t_len - 1::-1], pw_re[1, :, :t_len])
    p2 = lanes4(pw_im[0, :, t_len - 1::-1], pw_im[1, :, :t_len])
    x1 = jnp.concatenate([bt[0].real, bt[1].real, bt[0].imag, bt[1].imag], axis=-1)
    x2 = jnp.concatenate([-bt[0].imag, -bt[1].imag, bt[0].real, bt[1].real], axis=-1)
    q1 = lanes4(pw_re[0, :, 1:t_len + 1], pw_re[1, :, t_len:0:-1])
    q2 = lanes4(pw_im[0, :, 1:t_len + 1], pw_im[1, :, t_len:0:-1])
    y1 = jnp.concatenate([cc[0].real, cc[1].real, -cc[0].imag, -cc[1].imag], axis=-1)
    y2 = jnp.concatenate([-cc[0].imag, -cc[1].imag, -cc[0].real, -cc[1].real], axis=-1)
    rows = jnp.stack([p1, p2, q1, q2], axis=1)
    cols = jnp.stack([x1, x2, y1, y2], axis=1)

    dre = jnp.concatenate([pw_re[0, :, t_len], pw_re[1, :, t_len]], axis=-1)
    dim = jnp.concatenate([pw_im[0, :, t_len], pw_im[1, :, t_len]], axis=-1)
    decay = jnp.stack([dre, dim], axis=1)
    return strip.astype(F32), rows.astype(F32), cols.astype(F32), decay.astype(F32)


def _s5_kernel(u_ref, strip_ref, rows_ref, cols_ref, dec_ref, dskip_ref, y_ref,
               mt_ref, strips_ref, wst_ref, clt_ref, dsre_ref, dsim_ref, fre_ref, fim_ref, bre_ref, bim_ref,
               *, batch, n_ctx_steps, n_lat_steps):
    t_len = CHUNK
    width = t_len * S5_GW
    lanes = 2 * S5_STATE
    strip = strip_ref[0]
    for r in range(8):
        strips_ref[r] = strip[:, r * S5_GW:r * S5_GW + 2 * width].astype(BF16)
    for s in range(t_len):
        off = (t_len - 1 - s) * S5_GW
        r, q = (off // S5_GW) % 8, off // 128
        mt_ref[s * S5_GW:(s + 1) * S5_GW, :] = strips_ref[r, :, q * 128:q * 128 + width]
    for dst_ref, k in ((wst_ref, 0), (clt_ref, 2)):
        col_a, col_b = cols_ref[0, k], cols_ref[0, k + 1]
        for s in range(t_len):
            blk = col_a * rows_ref[0, k, s:s + 1, :] + col_b * rows_ref[0, k + 1, s:s + 1, :]
            dst_ref[s * S5_GW:(s + 1) * S5_GW, :] = blk.astype(BF16)
    u = u_ref[0]
    ds = jnp.dot(u, wst_ref[...], preferred_element_type=F32)
    dsre_ref[...] = ds[:, :lanes]
    dsim_ref[...] = ds[:, lanes:]

    dre = jnp.broadcast_to(dec_ref[0, 0:1, :], (batch, lanes))
    dim = jnp.broadcast_to(dec_ref[0, 1:2, :], (batch, lanes))
    is_fwd = lax.broadcasted_iota(jnp.int32, (batch, lanes), 1) < S5_STATE

    def run(first_row, n_steps, carry):
        def step(it, carry):
            sre, sim = carry
            rows_f = pl.ds(first_row + it, batch, stride=n_steps)
            rows_b = pl.ds(first_row + n_steps - 1 - it, batch, stride=n_steps)
            fre_ref[rows_f, :] = sre
            fim_ref[rows_f, :] = sim
            bre_ref[rows_b, :] = sre
            bim_ref[rows_b, :] = sim
            add_re = jnp.where(is_fwd, dsre_ref[rows_f, :], dsre_ref[rows_b, :])
            add_im = jnp.where(is_fwd, dsim_ref[rows_f, :], dsim_ref[rows_b, :])
            return sre * dre - sim * dim + add_re, sre * dim + sim * dre + add_im

        return lax.fori_loop(0, n_steps, step, carry)

    zero = jnp.zeros((batch, lanes), F32)
    carry = run(0, n_ctx_steps, (zero, zero))
    run(batch * n_ctx_steps, n_lat_steps, carry)

    n_rows = dsre_ref.shape[0]
    fwd_rows = lax.broadcasted_iota(jnp.int32, (n_rows, lanes), 1) < S5_STATE
    sin = jnp.concatenate([jnp.where(fwd_rows, fre_ref[...], bre_ref[...]),
                           jnp.where(fwd_rows, fim_ref[...], bim_ref[...])], axis=1).astype(BF16)
    y = jnp.dot(u, mt_ref[...], preferred_element_type=F32)
    y = y + lax.dot_general(sin, clt_ref[...], (((1,), (1,)), ((), ())), preferred_element_type=F32)
    y_ref[0] = _gelu_tanh(y + dskip_ref[0] * u.astype(F32)).astype(y_ref.dtype)


def _s5_mix(ud, tables, d_skip, lay):
    b, s, n_ctx = lay["b"], lay["s"], lay["n_ctx"]
    strip, rows, cols, decay = tables
    width = CHUNK * S5_GW
    ncs, nls = n_ctx // CHUNK, s // CHUNK
    m = b * (ncs + nls)
    ug = ud.reshape(m, CHUNK, S5_GROUPS, S5_GW).transpose(2, 0, 1, 3).reshape(S5_GROUPS, m, width)
    dskip = jnp.broadcast_to(d_skip.astype(F32).reshape(S5_GROUPS, 1, 1, S5_GW),
                             (S5_GROUPS, 1, CHUNK, S5_GW)).reshape(S5_GROUPS, 1, width)
    state_rows = pltpu.VMEM((m, 2 * S5_STATE), F32)
    y = pl.pallas_call(
        functools.partial(_s5_kernel, batch=b, n_ctx_steps=ncs, n_lat_steps=nls),
        out_shape=jax.ShapeDtypeStruct((S5_GROUPS, m, width), BF16),
        grid=(S5_GROUPS,),
        in_specs=[
            pl.BlockSpec((1, m, width), lambda g: (g, 0, 0)),
            pl.BlockSpec((1, S5_GW, 2 * width + 128), lambda g: (g, 0, 0)),
            pl.BlockSpec((1, 4, CHUNK, 4 * S5_STATE), lambda g: (g, 0, 0, 0)),
            pl.BlockSpec((1, 4, S5_GW, 4 * S5_STATE), lambda g: (g, 0, 0, 0)),
            pl.BlockSpec((1, 2, 2 * S5_STATE), lambda g: (g, 0, 0)),
            pl.BlockSpec((1, 1, width), lambda g: (g, 0, 0)),
        ],
        out_specs=pl.BlockSpec((1, m, width), lambda g: (g, 0, 0)),
        scratch_shapes=[
            pltpu.VMEM((width, width), BF16),
            pltpu.VMEM((8, S5_GW, 2 * width), BF16),
            pltpu.VMEM((width, 4 * S5_STATE), BF16),
            pltpu.VMEM((width, 4 * S5_STATE), BF16),
            state_rows, state_rows, state_rows, state_rows, state_rows, state_rows,
        ],
        compiler_params=_cparams("arbitrary"),
        name="s5_scan",
    )(ug, strip, rows, cols, decay, dskip)
    return y.reshape(S5_GROUPS, m, CHUNK, S5_GW).transpose(1, 2, 0, 3).reshape(m * CHUNK, D_BRANCH)


def _s5_readout_kernel(y_ref, w_ref, o_ref):
    r = jnp.dot(y_ref[...], w_ref[...], preferred_element_type=F32)
    o_ref[...] = (r[:, :D_BRANCH] * _sigmoid(r[:, D_BRANCH:])).astype(o_ref.dtype)


def _s5_readout(y, w_glu, layer, lay):
    n_tok = y.shape[0]
    rows = lay["tm"]
    return pl.pallas_call(
        _s5_readout_kernel,
        out_shape=jax.ShapeDtypeStruct((n_tok, D_BRANCH), BF16),
        grid=(n_tok // rows,),
        in_specs=[
            pl.BlockSpec((rows, D_BRANCH), lambda i: (i, 0)),
            pl.BlockSpec((None, D_BRANCH, 2 * D_BRANCH), lambda i: (layer, 0, 0)),
        ],
        out_specs=pl.BlockSpec((rows, D_BRANCH), lambda i: (i, 0)),
        compiler_params=_cparams("arbitrary"),
        name="s5_readout",
    )(y, w_glu)


def _merge_kernel(h_ref, ba_ref, bb_ref, bc_ref, bd_ref, wg_ref, bg_ref, wb_ref, o_ref):
    h = h_ref[...]
    acc = None
    for k, br_ref in enumerate((ba_ref, bb_ref, bc_ref, bd_ref)):
        gate = _sigmoid(jnp.dot(h, wg_ref[k], preferred_element_type=F32) + bg_ref[k])
        term = gate * jnp.dot(br_ref[...], wb_ref[k], preferred_element_type=F32)
        acc = term if acc is None else acc + term
    o_ref[...] = acc.astype(o_ref.dtype)


def _merge(h, branches, w_gate, b_gate, w_branch, layer, first, lay):
    n_tok, d = h.shape
    tm, tn = lay["tm"], 512
    br_spec = pl.BlockSpec((tm, D_BRANCH), lambda i, j: (i + first, 0))
    return pl.pallas_call(
        _merge_kernel,
        out_shape=jax.ShapeDtypeStruct((n_tok, d), BF16),
        grid=(n_tok // tm - first, d // tn),
        in_specs=[
            pl.BlockSpec((tm, d), lambda i, j: (i + first, 0)),
            br_spec, br_spec, br_spec, br_spec,
            pl.BlockSpec((None, N_BRANCH, d, tn), lambda i, j: (layer, 0, 0, j)),
            pl.BlockSpec((N_BRANCH, 1, tn), lambda i, j: (0, 0, j)),
            pl.BlockSpec((None, N_BRANCH, D_BRANCH, tn), lambda i, j: (layer, 0, 0, j)),
        ],
        out_specs=pl.BlockSpec((tm, tn), lambda i, j: (i + first, j)),
        compiler_params=_cparams("arbitrary", "arbitrary"),
        name="branch_merge",
    )(h, *branches, w_gate, b_gate.reshape(N_BRANCH, 1, d), w_branch)


def _outproj_kernel(m_ref, w_ref, x_ref, gate_ref, o_ref):
    o_ref[...] = x_ref[...] + gate_ref[0] * jnp.dot(m_ref[...], w_ref[...], preferred_element_type=F32)


def _out_projection(merged, w_out, xall, mod3, layer, first, lay):
    n_tok, d = xall.shape
    split = 2
    tm = lay["tm"] // split
    mod_row = lay["mod_row"]
    skip = first * split
    return pl.pallas_call(
        _outproj_kernel,
        out_shape=jax.ShapeDtypeStruct((n_tok, d), F32),
        grid=(n_tok // tm - skip,),
        in_specs=[
            pl.BlockSpec((tm, d), lambda i: (i + skip, 0)),
            pl.BlockSpec((None, d, d), lambda i: (layer, 0, 0)),
            pl.BlockSpec((tm, d), lambda i: (i + skip, 0)),
            pl.BlockSpec((1, 1, d), lambda i: (mod_row((i + skip) // split, 2), 0, 0)),
        ],
        out_specs=pl.BlockSpec((tm, d), lambda i: (i + skip, 0)),
        compiler_params=_cparams("arbitrary"),
        name="out_projection",
    )(merged, w_out, xall, mod3)


def _ffn_kernel(x_ref, g_ref, shift_ref, scale_ref, gate_ref, w1_ref, w2_ref, o_ref, h_ref):
    j = pl.program_id(1)

    @pl.when(j == 0)
    def _():
        _modulated_rms_norm(x_ref, g_ref, shift_ref, scale_ref, h_ref)
        o_ref[...] = jnp.zeros_like(o_ref)

    a = jnp.maximum(jnp.dot(h_ref[...], w1_ref[...], preferred_element_type=F32), 0.0)
    o_ref[...] += jnp.dot((a * a).astype(BF16), w2_ref[...], preferred_element_type=F32)

    @pl.when(j == pl.num_programs(1) - 1)
    def _():
        o_ref[...] = x_ref[...] + gate_ref[0] * o_ref[...]


def _ffn(xall, norm_g, mod3, w1, w2, layer, first, lay):
    n_tok, d = xall.shape
    tm, tf = lay["tm"], 512
    mod_row = lay["mod_row"]
    mspec = lambda which: pl.BlockSpec((1, 1, d), lambda i, j: (mod_row(i + first, which), 0, 0))
    return pl.pallas_call(
        _ffn_kernel,
        out_shape=jax.ShapeDtypeStruct((n_tok, d), F32),
        grid=(n_tok // tm - first, D_FF // tf),
        in_specs=[
            pl.BlockSpec((tm, d), lambda i, j: (i + first, 0)),
            pl.BlockSpec((1, d), lambda i, j: (0, 0)),
            mspec(3), mspec(4), mspec(5),
            pl.BlockSpec((None, d, tf), lambda i, j: (layer, 0, j)),
            pl.BlockSpec((None, tf, d), lambda i, j: (layer, j, 0)),
        ],
        out_specs=pl.BlockSpec((tm, d), lambda i, j: (i + first, 0)),
        scratch_shapes=[pltpu.VMEM((tm, d), BF16)],
        compiler_params=_cparams("arbitrary", "arbitrary"),
        name="ffn",
    )(xall, norm_g.reshape(1, d), mod3, mod3, mod3, w1, w2)


def _final_norm_kernel(x_ref, g_ref, o_ref):
    x = x_ref[...]
    o_ref[...] = x * lax.rsqrt(jnp.mean(x * x, axis=-1, keepdims=True) + EPS) * g_ref[...]


def _final_norm(xall, g, lay):
    d = xall.shape[1]
    tm = lay["tm"]
    first = lay["nc"] // tm
    n_lat = lay["b"] * lay["s"]
    return pl.pallas_call(
        _final_norm_kernel,
        out_shape=jax.ShapeDtypeStruct((n_lat, d), F32),
        grid=(n_lat // tm,),
        in_specs=[pl.BlockSpec((tm, d), lambda i: (i + first, 0)), pl.BlockSpec((1, d), lambda i: (0, 0))],
        out_specs=pl.BlockSpec((tm, d), lambda i: (i, 0)),
        compiler_params=_cparams("arbitrary"),
        name="final_norm",
    )(xall, g.reshape(1, d))


def _reorder_in_columns(w_in):
    q_end = 5 * D_BRANCH
    kv = 2 * N_KV_HEADS * HEAD_DIM
    pad = jnp.zeros(w_in.shape[:-1] + (IN_COLS_PAD - IN_COLS,), w_in.dtype)
    return jnp.concatenate([w_in[..., :q_end], w_in[..., q_end + kv:], w_in[..., q_end:q_end + kv], pad], axis=-1)


def kernel(x, c, ctx, c_ctx, w_mod, b_mod, norm1_g, norm2_g, w_in, gmlp_ln_g, gmlp_ln_b, gmlp_ws, gmlp_bs, conv_w, conv_b, conv_ln_g, conv_ln_b, attn_sink, s5_a_re, s5_a_im, s5_log_step, s5_b_re, s5_b_im, s5_c_re, s5_c_im, s5_d, s5_w_glu, w_branch, w_gate, b_gate, w_out, w_ff1, w_ff2, final_g):
    b, s, d = x.shape
    n_ctx = ctx.shape[1]
    depth = w_mod.shape[0]
    assert d == D_MODEL and s % GRID_W == 0
    nc = b * n_ctx
    tm = _row_tile(nc, s)
    tseq = min(256, tm)
    nct, lpt = nc // tm, s // tm

    def mod_row(i, which):
        bidx = jnp.where(i < nct, b, (i - nct) // lpt)
        return bidx * N_MOD + which

    lay = dict(b=b, s=s, n_ctx=n_ctx, nc=nc, tm=tm, tseq=tseq, mod_row=mod_row)

    cond = jnp.concatenate([c, c_ctx[None, :]], axis=0)
    cond = jnp.pad(cond, ((0, -cond.shape[0] % 8), (0, 0)))
    mod_all = _modulation(cond, w_mod, b_mod)
    rope_tab = _rope_tables(s)
    xall = jnp.concatenate([ctx.reshape(nc, d), x.reshape(b * s, d)], axis=0)

    w_in_b = _reorder_in_columns(w_in).astype(BF16)
    w_gate_b, w_branch_b, w_out_b = w_gate.astype(BF16), w_branch.astype(BF16), w_out.astype(BF16)
    w_ff1_b, w_ff2_b, w_glu_b = w_ff1.astype(BF16), w_ff2.astype(BF16), s5_w_glu.astype(BF16)

    for l in range(depth):
        first = nct if l == depth - 1 else 0
        mod3 = mod_all[l, :b + 1].reshape((b + 1) * N_MOD, 1, d)
        h, z, ud = _in_projection(xall, norm1_g[l], mod3, w_in_b, l, lay)
        br_a = _gmlp(z, gmlp_ln_g[l], gmlp_ln_b[l], gmlp_ws[l], gmlp_bs[l], lay)
        br_b = _conformer_conv(z, conv_w[l], conv_b[l], conv_ln_g[l], conv_ln_b[l], lay)
        br_c = _attention(z, attn_sink[l], rope_tab, lay)
        tables = _s5_tables(s5_a_re[l], s5_a_im[l], s5_log_step[l], s5_b_re[l], s5_b_im[l], s5_c_re[l], s5_c_im[l])
        y = _s5_mix(ud, tables, s5_d[l], lay)
        br_d = _s5_readout(y, w_glu_b, l, lay)
        merged = _merge(h, (br_a, br_b, br_c, br_d), w_gate_b, b_gate[l], w_branch_b, l, first, lay)
        xall = _out_projection(merged, w_out_b, xall, mod3, l, first, lay)
        xall = _ffn(xall, norm2_g[l], mod3, w_ff1_b, w_ff2_b, l, first, lay)

    return _final_norm(xall, final_g, lay).reshape(b, s, d)
```

```python
import functools
import math

import jax
import jax.numpy as jnp
from jax import lax
from jax.experimental import pallas as pl
from jax.experimental.pallas import tpu as pltpu

D_MODEL = 2048
N_BRANCH = 4
D_BRANCH = D_MODEL // N_BRANCH
CHUNK = 128
GMLP_GROUPS = 4
CONV_W = 31
CONV_HALO = 16
HEAD_DIM = 64
N_Q_HEADS = D_BRANCH // HEAD_DIM
N_KV_HEADS = 2
Q_PER_KV = N_Q_HEADS // N_KV_HEADS
WINDOW = 128
ATTN_Q_BLOCKS = 8
ROPE_BASE = 10000.0
GRID_W = 64
S5_GW = 16
S5_GROUPS = D_BRANCH // S5_GW
S5_STATE = 64
D_FF = 4 * D_MODEL
N_MOD = 6
EPS = 1e-6
NEG_INF = -1e30

COL_A = 0
COL_B = 2 * D_BRANCH
COL_Q = 4 * D_BRANCH
COL_D = 5 * D_BRANCH
COL_K = 6 * D_BRANCH
COL_V = COL_K + N_KV_HEADS * HEAD_DIM
IN_COLS = COL_V + N_KV_HEADS * HEAD_DIM
IN_COLS_PAD = -(-IN_COLS // D_BRANCH) * D_BRANCH

VMEM_LIMIT_V7X = 56 * 1024 * 1024

F32 = jnp.float32
BF16 = jnp.bfloat16


def _cparams(*sem):
    return pltpu.CompilerParams(dimension_semantics=sem, vmem_limit_bytes=VMEM_LIMIT_V7X)


def _gelu_tanh(x):
    return 0.5 * x * (1.0 + jnp.tanh(math.sqrt(2.0 / math.pi) * (x + 0.044715 * (x * x * x))))


def _sigmoid(x):
    return 1.0 / (1.0 + jnp.exp(-x))


def _layer_norm_rows(v, g, b):
    mu = jnp.mean(v, axis=-1, keepdims=True)
    vc = v - mu
    var = jnp.mean(vc * vc, axis=-1, keepdims=True)
    return vc * lax.rsqrt(var + EPS) * g + b


def _row_tile(nc, s):
    for t in (1024, 512, 256, 128):
        if nc % t == 0 and s % t == 0:
            return t
    raise ValueError("context and sequence lengths must be multiples of 128")


def _mod_kernel(c_ref, w_ref, b_ref, o_ref):
    c = c_ref[...]
    c = (c * _sigmoid(c)).astype(BF16)
    o_ref[0] = jnp.dot(c, w_ref[0].astype(BF16), preferred_element_type=F32) + b_ref[0]


def _modulation(cond, w_mod, b_mod):
    depth, d, n = w_mod.shape
    rows = cond.shape[0]
    tn = 1024
    return pl.pallas_call(
        _mod_kernel,
        out_shape=jax.ShapeDtypeStruct((depth, rows, n), F32),
        grid=(depth, n // tn),
        in_specs=[
            pl.BlockSpec((rows, d), lambda l, j: (0, 0)),
            pl.BlockSpec((1, d, tn), lambda l, j: (l, 0, j)),
            pl.BlockSpec((1, 1, tn), lambda l, j: (l, 0, j)),
        ],
        out_specs=pl.BlockSpec((1, rows, tn), lambda l, j: (l, 0, j)),
        compiler_params=_cparams("arbitrary", "arbitrary"),
        name="modulation",
    )(cond, w_mod, b_mod.reshape(depth, 1, n))


NORM_ROWS = 128


def _modulated_rms_norm(x_ref, g_ref, shift_ref, scale_ref, h_ref):
    gain = g_ref[...] * (1.0 + scale_ref[0])
    shift = shift_ref[0]

    def body(r, carry):
        rows = pl.ds(pl.multiple_of(r * NORM_ROWS, NORM_ROWS), NORM_ROWS)
        x = x_ref[rows, :]
        inv = lax.rsqrt(jnp.mean(x * x, axis=-1, keepdims=True) + EPS)
        h_ref[rows, :] = (x * inv * gain + shift).astype(h_ref.dtype)
        return carry

    lax.fori_loop(0, x_ref.shape[0] // NORM_ROWS, body, 0)


def _inproj_kernel(x_ref, g_ref, shift_ref, scale_ref, w_ref, h_ref, z_ref, ud_ref):
    _modulated_rms_norm(x_ref, g_ref, shift_ref, scale_ref, h_ref)
    h = h_ref[...]
    for c0 in range(0, z_ref.shape[1], D_BRANCH):
        z = jnp.dot(h, w_ref[:, c0:c0 + D_BRANCH], preferred_element_type=F32).astype(z_ref.dtype)
        z_ref[:, c0:c0 + D_BRANCH] = z
        if c0 == COL_D:
            ud_ref[...] = z


def _in_projection(xall, norm_g, mod3, w_in, layer, lay):
    n_tok, d = xall.shape
    split = 2
    tm = lay["tm"] // split
    mod_row = lay["mod_row"]
    return pl.pallas_call(
        _inproj_kernel,
        out_shape=(jax.ShapeDtypeStruct((n_tok, d), BF16), jax.ShapeDtypeStruct((n_tok, IN_COLS_PAD), BF16),
                   jax.ShapeDtypeStruct((n_tok, D_BRANCH), BF16)),
        grid=(n_tok // tm,),
        in_specs=[
            pl.BlockSpec((tm, d), lambda i: (i, 0)),
            pl.BlockSpec((1, d), lambda i: (0, 0)),
            pl.BlockSpec((1, 1, d), lambda i: (mod_row(i // split, 0), 0, 0)),
            pl.BlockSpec((1, 1, d), lambda i: (mod_row(i // split, 1), 0, 0)),
            pl.BlockSpec((None, d, IN_COLS_PAD), lambda i: (layer, 0, 0)),
        ],
        out_specs=(pl.BlockSpec((tm, d), lambda i: (i, 0)), pl.BlockSpec((tm, IN_COLS_PAD), lambda i: (i, 0)),
                   pl.BlockSpec((tm, D_BRANCH), lambda i: (i, 0))),
        compiler_params=_cparams("arbitrary"),
        name="in_projection",
    )(xall, norm_g.reshape(1, d), mod3, mod3, w_in)


def _gmlp_kernel(za_ref, lng_ref, lnb_ref, ws_ref, bs_ref, o_ref):
    rows = za_ref.shape[0]
    gw = D_BRANCH // GMLP_GROUPS
    a = _gelu_tanh(za_ref[...].astype(F32))
    u = a[:, :D_BRANCH]
    v = _layer_norm_rows(a[:, D_BRANCH:], lng_ref[...], lnb_ref[...]).astype(BF16)
    for c in range(rows // CHUNK):
        r0 = c * CHUNK
        for g in range(GMLP_GROUPS):
            c0 = g * gw
            mixed = jnp.dot(ws_ref[g], v[r0:r0 + CHUNK, c0:c0 + gw], preferred_element_type=F32) + bs_ref[g]
            o_ref[r0:r0 + CHUNK, c0:c0 + gw] = (u[r0:r0 + CHUNK, c0:c0 + gw] * mixed).astype(o_ref.dtype)


def _gmlp(z, ln_g, ln_b, ws, bs, lay):
    n_tok = z.shape[0]
    rows = lay["tm"]
    gw = D_BRANCH // GMLP_GROUPS
    bs_full = jnp.broadcast_to(bs[:, :, None], (GMLP_GROUPS, CHUNK, gw)).astype(F32)
    return pl.pallas_call(
        _gmlp_kernel,
        out_shape=jax.ShapeDtypeStruct((n_tok, D_BRANCH), BF16),
        grid=(n_tok // rows,),
        in_specs=[
            pl.BlockSpec((rows, 2 * D_BRANCH), lambda i: (i, COL_A // (2 * D_BRANCH))),
            pl.BlockSpec((1, D_BRANCH), lambda i: (0, 0)),
            pl.BlockSpec((1, D_BRANCH), lambda i: (0, 0)),
            pl.BlockSpec((GMLP_GROUPS, CHUNK, CHUNK), lambda i: (0, 0, 0)),
            pl.BlockSpec((GMLP_GROUPS, CHUNK, gw), lambda i: (0, 0, 0)),
        ],
        out_specs=pl.BlockSpec((rows, D_BRANCH), lambda i: (i, 0)),
        compiler_params=_cparams("arbitrary"),
        name="gmlp_mix",
    )(z, ln_g.reshape(1, -1), ln_b.reshape(1, -1), ws.astype(BF16), bs_full)


def _conv_kernel(zm_ref, zp_ref, zn_ref, w_ref, cb_ref, lng_ref, lnb_ref, o_ref, ybuf, *, seg_tiles):
    rows = zm_ref.shape[0]
    i = pl.program_id(0)
    nct, cpt, lpt = seg_tiles
    pos = jnp.where(i < nct, i % cpt, (i - nct) % lpt)
    seg = jnp.where(i < nct, cpt, lpt)
    keep_prev = (pos != 0).astype(F32)
    keep_next = (pos != seg - 1).astype(F32)

    def glu(ref):
        z = ref[...].astype(F32)
        return z[:, :D_BRANCH] * _sigmoid(z[:, D_BRANCH:])

    ybuf[0:CONV_HALO, :] = glu(zp_ref) * keep_prev
    ybuf[CONV_HALO:CONV_HALO + rows, :] = glu(zm_ref)
    ybuf[CONV_HALO + rows:, :] = glu(zn_ref) * keep_next

    sub, tile = 32, 8
    off = CONV_HALO - CONV_W // 2
    for r in range(rows // sub):
        acc = jnp.zeros((sub, D_BRANCH), F32)
        for res in range(tile):
            part = None
            for k in range(CONV_W):
                if (off + k) % tile != res:
                    continue
                start = r * sub + off + k - res
                term = ybuf[start:start + sub + tile, :] * w_ref[k:k + 1, :]
                part = term if part is None else part + term
            if part is not None:
                acc = acc + part[res:res + sub, :]
        y = _layer_norm_rows(acc + cb_ref[...], lng_ref[...], lnb_ref[...])
        o_ref[r * sub:(r + 1) * sub, :] = (y * _sigmoid(y)).astype(o_ref.dtype)


def _conformer_conv(z, w_dw, b_dw, ln_g, ln_b, lay):
    n_tok = z.shape[0]
    rows = lay["tseq"]
    hb = rows // CONV_HALO
    n_halo_blocks = n_tok // CONV_HALO
    colb = COL_B // (2 * D_BRANCH)
    seg_tiles = (lay["nc"] // rows, lay["n_ctx"] // rows, lay["s"] // rows)
    vec = lambda a: a.reshape(1, -1)
    return pl.pallas_call(
        functools.partial(_conv_kernel, seg_tiles=seg_tiles),
        out_shape=jax.ShapeDtypeStruct((n_tok, D_BRANCH), BF16),
        grid=(n_tok // rows,),
        in_specs=[
            pl.BlockSpec((rows, 2 * D_BRANCH), lambda i: (i, colb)),
            pl.BlockSpec((CONV_HALO, 2 * D_BRANCH), lambda i: (jnp.maximum(i * hb - 1, 0), colb)),
            pl.BlockSpec((CONV_HALO, 2 * D_BRANCH), lambda i: (jnp.minimum((i + 1) * hb, n_halo_blocks - 1), colb)),
            pl.BlockSpec((CONV_W, D_BRANCH), lambda i: (0, 0)),
            pl.BlockSpec((1, D_BRANCH), lambda i: (0, 0)),
            pl.BlockSpec((1, D_BRANCH), lambda i: (0, 0)),
            pl.BlockSpec((1, D_BRANCH), lambda i: (0, 0)),
        ],
        out_specs=pl.BlockSpec((rows, D_BRANCH), lambda i: (i, 0)),
        scratch_shapes=[pltpu.VMEM((rows + 2 * CONV_HALO, D_BRANCH), F32)],
        compiler_params=_cparams("arbitrary"),
        name="conformer_conv",
    )(z, z, z, w_dw, vec(b_dw), vec(ln_g), vec(ln_b))


def _rope(x, cos, sin_a, sin_b):
    reps = x.shape[1] // cos.shape[1]
    if reps > 1:
        cos, sin_a, sin_b = (jnp.concatenate([t] * reps, axis=1) for t in (cos, sin_a, sin_b))
    n = x.shape[1]
    half = HEAD_DIM // 4
    return x * cos + pltpu.roll(x, n - half, 1) * sin_a + pltpu.roll(x, half, 1) * sin_b


def _attn_core(q, keys, vals, bias, sink_ref, o_ref, r0=0):
    rows = q.shape[0]
    rid = lax.broadcasted_iota(jnp.int32, (Q_PER_KV * rows, 1), 0)
    for h in range(N_KV_HEADS):
        kh = keys[:, h * HEAD_DIM:(h + 1) * HEAD_DIM]
        vh = vals[:, h * HEAD_DIM:(h + 1) * HEAD_DIM]
        heads = [q[:, (h * Q_PER_KV + g) * HEAD_DIM:(h * Q_PER_KV + g + 1) * HEAD_DIM] for g in range(Q_PER_KV)]
        qh = jnp.concatenate(heads, axis=0).astype(BF16)
        s = lax.dot_general(qh, kh, (((1,), (1,)), ((), ())), preferred_element_type=F32)
        if bias is not None:
            s = s + jnp.concatenate([bias] * Q_PER_KV, axis=0)
        sink = jnp.full((Q_PER_KV * rows, 1), sink_ref[h * Q_PER_KV], F32)
        for g in range(1, Q_PER_KV):
            sink = jnp.where(rid >= g * rows, sink_ref[h * Q_PER_KV + g], sink)
        m = jnp.maximum(jnp.max(s, axis=-1, keepdims=True), sink)
        e = jnp.exp(s - m)
        denom = jnp.sum(e, axis=-1, keepdims=True) + jnp.exp(sink - m)
        o = jnp.dot(e.astype(BF16), vh, preferred_element_type=F32) * (1.0 / denom)
        for g in range(Q_PER_KV):
            c0 = (h * Q_PER_KV + g) * HEAD_DIM
            o_ref[r0:r0 + rows, c0:c0 + HEAD_DIM] = o[g * rows:(g + 1) * rows, :].astype(o_ref.dtype)


def _attn_latent_kernel(sink_ref, q_ref, kp_ref, kc_ref, kn_ref, vp_ref, vc_ref, vn_ref, kx_ref, vx_ref,
                        cq_ref, cp_ref, cn_ref, o_ref, *, n_blocks):
    m = pl.program_id(1)
    rows = CHUNK
    scale = HEAD_DIM ** -0.5

    def tables(ref):
        return ref[0], ref[1], ref[2]

    def split(x):
        return [x[i * rows:(i + 1) * rows] for i in range(ATTN_Q_BLOCKS)]

    f32 = lambda ref: ref[...].astype(F32)
    q = _rope(f32(q_ref), *tables(cq_ref)) * scale
    k_blocks = ([_rope(f32(kp_ref), *tables(cp_ref)).astype(BF16)]
                + split(_rope(f32(kc_ref), *tables(cq_ref)).astype(BF16))
                + [_rope(f32(kn_ref), *tables(cn_ref)).astype(BF16)])
    v_blocks = [vp_ref[...].astype(BF16)] + split(vc_ref[...].astype(BF16)) + [vn_ref[...].astype(BF16)]
    kx, vx = kx_ref[...].astype(BF16), vx_ref[...].astype(BF16)
    nk = 3 * rows + kx.shape[0]
    qi = lax.broadcasted_iota(jnp.int32, (rows, nk), 0)
    kj = lax.broadcasted_iota(jnp.int32, (rows, nk), 1)
    rel = kj - qi
    for half in range(ATTN_Q_BLOCKS):
        n = ATTN_Q_BLOCKS * m + half
        keys = jnp.concatenate(k_blocks[half:half + 3] + [kx], axis=0)
        vals = jnp.concatenate(v_blocks[half:half + 3] + [vx], axis=0)
        lo = jnp.where(n > 0, 0, rows)
        hi = jnp.where(n < n_blocks - 1, 3 * rows, 2 * rows)
        ok = (rel >= 0) & (rel <= 2 * WINDOW) & (kj >= lo) & (kj < hi)
        ok = ok | (kj >= 3 * rows)
        bias = jnp.where(ok, 0.0, NEG_INF).astype(F32)
        _attn_core(q[half * rows:(half + 1) * rows], keys, vals, bias, sink_ref, o_ref, r0=half * rows)


def _attn_ctx_kernel(sink_ref, q_ref, kx_ref, vx_ref, latent_out_ref, o_ref):
    del latent_out_ref
    q = q_ref[...].astype(F32) * (HEAD_DIM ** -0.5)
    _attn_core(q, kx_ref[...].astype(BF16), vx_ref[...].astype(BF16), None, sink_ref, o_ref)


def _rope_tables(s):
    t = jnp.arange(s)
    row = (t // GRID_W).astype(F32)[:, None]
    col = (t % GRID_W).astype(F32)[:, None]
    half = HEAD_DIM // 2
    inv = ROPE_BASE ** (-jnp.arange(0, half, 2, dtype=F32) / half)
    lane = jnp.arange(2 * HEAD_DIM)
    d = lane % HEAD_DIM
    freq = inv[(d % half) % (half // 2)][None, :]
    ang = jnp.where((d < half)[None, :], row * freq, col * freq)
    first = ((d % half) < half // 2)[None, :]
    sin = jnp.sin(ang)
    return jnp.stack([jnp.cos(ang), jnp.where(first, -sin, 0.0), jnp.where(first, 0.0, sin)]).astype(F32)


def _attention(z, sink, rope_tab, lay):
    n_tok = z.shape[0]
    b, s, n_ctx, nc = lay["b"], lay["s"], lay["n_ctx"], lay["nc"]
    nb = s // CHUNK
    cb0 = nc // CHUNK
    kvw = N_KV_HEADS * HEAD_DIM
    qcol, kcol, vcol = COL_Q // D_BRANCH, COL_K // kvw, COL_V // kvw
    nq = ATTN_Q_BLOCKS
    assert nb % nq == 0 and cb0 % nq == 0
    pair = nq * CHUNK
    lat = lambda bb, n: cb0 + bb * nb + n
    lat2 = lambda bb, m: cb0 // nq + bb * (nb // nq) + m
    prev = lambda m: jnp.maximum(nq * m - 1, 0)
    nxt = lambda m: jnp.minimum(nq * m + nq, nb - 1)
    smem = pl.BlockSpec(memory_space=pltpu.SMEM)
    out_lat = pl.pallas_call(
        functools.partial(_attn_latent_kernel, n_blocks=nb),
        out_shape=jax.ShapeDtypeStruct((n_tok, D_BRANCH), BF16),
        grid=(b, nb // nq),
        in_specs=[
            smem,
            pl.BlockSpec((pair, D_BRANCH), lambda bb, m: (lat2(bb, m), qcol)),
            pl.BlockSpec((CHUNK, kvw), lambda bb, m: (lat(bb, prev(m)), kcol)),
            pl.BlockSpec((pair, kvw), lambda bb, m: (lat2(bb, m), kcol)),
            pl.BlockSpec((CHUNK, kvw), lambda bb, m: (lat(bb, nxt(m)), kcol)),
            pl.BlockSpec((CHUNK, kvw), lambda bb, m: (lat(bb, prev(m)), vcol)),
            pl.BlockSpec((pair, kvw), lambda bb, m: (lat2(bb, m), vcol)),
            pl.BlockSpec((CHUNK, kvw), lambda bb, m: (lat(bb, nxt(m)), vcol)),
            pl.BlockSpec((n_ctx, kvw), lambda bb, m: (bb, kcol)),
            pl.BlockSpec((n_ctx, kvw), lambda bb, m: (bb, vcol)),
            pl.BlockSpec((3, pair, kvw), lambda bb, m: (0, m, 0)),
            pl.BlockSpec((3, CHUNK, kvw), lambda bb, m: (0, prev(m), 0)),
            pl.BlockSpec((3, CHUNK, kvw), lambda bb, m: (0, nxt(m), 0)),
        ],
        out_specs=pl.BlockSpec((pair, D_BRANCH), lambda bb, m: (lat2(bb, m), 0)),
        compiler_params=_cparams("arbitrary", "arbitrary"),
        name="attention_latent",
    )(sink, z, z, z, z, z, z, z, z, z, rope_tab, rope_tab, rope_tab)
    ncb = n_ctx // CHUNK
    return pl.pallas_call(
        _attn_ctx_kernel,
        out_shape=jax.ShapeDtypeStruct((n_tok, D_BRANCH), BF16),
        grid=(b, ncb),
        in_specs=[
            smem,
            pl.BlockSpec((CHUNK, D_BRANCH), lambda bb, n: (bb * ncb + n, qcol)),
            pl.BlockSpec((n_ctx, kvw), lambda bb, n: (bb, kcol)),
            pl.BlockSpec((n_ctx, kvw), lambda bb, n: (bb, vcol)),
            pl.BlockSpec(memory_space=pl.ANY),
        ],
        out_specs=pl.BlockSpec((CHUNK, D_BRANCH), lambda bb, n: (bb * ncb + n, 0)),
        input_output_aliases={4: 0},
        compiler_params=_cparams("arbitrary", "arbitrary"),
        name="attention_ctx",
    )(sink, z, z, z, out_lat)


def _s5_tables(a_re, a_im, log_step, b_re, b_im, c_re, c_im):
    t_len = CHUNK
    hi = lax.Precision.HIGHEST
    j = jnp.arange(t_len + 1, dtype=F32)
    lam = lax.complex(a_re.astype(F32), a_im.astype(F32))
    log_lbar = lam * jnp.exp(log_step.astype(F32))[..., None]
    lbar = jnp.exp(log_lbar)
    b = lax.complex(b_re.astype(F32), b_im.astype(F32))
    bbar = ((lbar - 1.0) / lam)[..., None] * b[None]
    cc = lax.complex(c_re.astype(F32), c_im.astype(F32))
    pw = jnp.exp(log_lbar[:, :, None, :] * j[None, None, :, None])
    pw_re, pw_im = pw.real, pw.imag

    q = cc[:, :, :, None, :] * bbar.transpose(0, 1, 3, 2)[:, :, None, :, :]
    taps = (jnp.einsum("dgcip,dgjp->dgjci", q.real, pw_re[:, :, :t_len], precision=hi)
            - jnp.einsum("dgcip,dgjp->dgjci", q.imag, pw_im[:, :, :t_len], precision=hi))
    kf, kb = taps[0], taps[1]
    kc = jnp.concatenate([kb[:, :0:-1], (kf[:, :1] + kb[:, :1]), kf[:, 1:]], axis=1)
    strip = kc.transpose(0, 3, 1, 2).reshape(S5_GROUPS, S5_GW, (2 * t_len - 1) * S5_GW)
    width = 2 * t_len * S5_GW
    strip = jnp.pad(strip, ((0, 0), (0, 0), (0, width + 128 - strip.shape[-1])))

    def lanes4(f, bk):
        return jnp.concatenate([f, bk, f, bk], axis=-1)

    bt = bbar.transpose(0, 1, 3, 2)
    p1 = lanes4(pw_re[0, :, t_len - 1::-1], pw_re[1, :, :t_len])
    p2 = lanes4(pw_im[0, :, t_len - 1::-1], pw_im[1, :, :t_len])
    x1 = jnp.concatenate([bt[0].real, bt[1].real, bt[0].imag, bt[1].imag], axis=-1)
    x2 = jnp.concatenate([-bt[0].imag, -bt[1].imag, bt[0].real, bt[1].real], axis=-1)
    q1 = lanes4(pw_re[0, :, 1:t_len + 1], pw_re[1, :, t_len:0:-1])
    q2 = lanes4(pw_im[0, :, 1:t_len + 1], pw_im[1, :, t_len:0:-1])
    y1 = jnp.concatenate([cc[0].real, cc[1].real, -cc[0].imag, -cc[1].imag], axis=-1)
    y2 = jnp.concatenate([-cc[0].imag, -cc[1].imag, -cc[0].real, -cc[1].real], axis=-1)
    rows = jnp.stack([p1, p2, q1, q2], axis=1)
    cols = jnp.stack([x1, x2, y1, y2], axis=1)

    dre = jnp.concatenate([pw_re[0, :, t_len], pw_re[1, :, t_len]], axis=-1)
    dim = jnp.concatenate([pw_im[0, :, t_len], pw_im[1, :, t_len]], axis=-1)
    decay = jnp.stack([dre, dim], axis=1)
    return strip.astype(F32), rows.astype(F32), cols.astype(F32), decay.astype(F32)


def _s5_kernel(u_ref, strip_ref, rows_ref, cols_ref, dec_ref, dskip_ref, y_ref,
               mt_ref, strips_ref, wst_ref, clt_ref, dsre_ref, dsim_ref, fre_ref, fim_ref, bre_ref, bim_ref,
               *, batch, n_ctx_steps, n_lat_steps):
    t_len = CHUNK
    width = t_len * S5_GW
    lanes = 2 * S5_STATE
    strip = strip_ref[0]
    for r in range(8):
        strips_ref[r] = strip[:, r * S5_GW:r * S5_GW + 2 * width].astype(BF16)
    for s in range(t_len):
        off = (t_len - 1 - s) * S5_GW
        r, q = (off // S5_GW) % 8, off // 128
        mt_ref[s * S5_GW:(s + 1) * S5_GW, :] = strips_ref[r, :, q * 128:q * 128 + width]
    for dst_ref, k in ((wst_ref, 0), (clt_ref, 2)):
        col_a, col_b = cols_ref[0, k], cols_ref[0, k + 1]
        for s in range(t_len):
            blk = col_a * rows_ref[0, k, s:s + 1, :] + col_b * rows_ref[0, k + 1, s:s + 1, :]
            dst_ref[s * S5_GW:(s + 1) * S5_GW, :] = blk.astype(BF16)
    u = u_ref[0]
    ds = jnp.dot(u, wst_ref[...], preferred_element_type=F32)
    dsre_ref[...] = ds[:, :lanes]
    dsim_ref[...] = ds[:, lanes:]

    dre = jnp.broadcast_to(dec_ref[0, 0:1, :], (batch, lanes))
    dim = jnp.broadcast_to(dec_ref[0, 1:2, :], (batch, lanes))
    is_fwd = lax.broadcasted_iota(jnp.int32, (batch, lanes), 1) < S5_STATE

    def run(first_row, n_steps, carry):
        def step(it, carry):
            sre, sim = carry
            rows_f = pl.ds(first_row + it, batch, stride=n_steps)
            rows_b = pl.ds(first_row + n_steps - 1 - it, batch, stride=n_steps)
            fre_ref[rows_f, :] = sre
            fim_ref[rows_f, :] = sim
            bre_ref[rows_b, :] = sre
            bim_ref[rows_b, :] = sim
            add_re = jnp.where(is_fwd, dsre_ref[rows_f, :], dsre_ref[rows_b, :])
            add_im = jnp.where(is_fwd, dsim_ref[rows_f, :], dsim_ref[rows_b, :])
            return sre * dre - sim * dim + add_re, sre * dim + sim * dre + add_im

        return lax.fori_loop(0, n_steps, step, carry)

    zero = jnp.zeros((batch, lanes), F32)
    carry = run(0, n_ctx_steps, (zero, zero))
    run(batch * n_ctx_steps, n_lat_steps, carry)

    n_rows = dsre_ref.shape[0]
    fwd_rows = lax.broadcasted_iota(jnp.int32, (n_rows, lanes), 1) < S5_STATE
    sin = jnp.concatenate([jnp.where(fwd_rows, fre_ref[...], bre_ref[...]),
                           jnp.where(fwd_rows, fim_ref[...], bim_ref[...])], axis=1).astype(BF16)
    y = jnp.dot(u, mt_ref[...], preferred_element_type=F32)
    y = y + lax.dot_general(sin, clt_ref[...], (((1,), (1,)), ((), ())), preferred_element_type=F32)
    y_ref[0] = _gelu_tanh(y + dskip_ref[0] * u.astype(F32)).astype(y_ref.dtype)


def _s5_mix(ud, tables, d_skip, lay):
    b, s, n_ctx = lay["b"], lay["s"], lay["n_ctx"]
    strip, rows, cols, decay = tables
    width = CHUNK * S5_GW
    ncs, nls = n_ctx // CHUNK, s // CHUNK
    m = b * (ncs + nls)
    ug = ud.reshape(m, CHUNK, S5_GROUPS, S5_GW).transpose(2, 0, 1, 3).reshape(S5_GROUPS, m, width)
    dskip = jnp.broadcast_to(d_skip.astype(F32).reshape(S5_GROUPS, 1, 1, S5_GW),
                             (S5_GROUPS, 1, CHUNK, S5_GW)).reshape(S5_GROUPS, 1, width)
    state_rows = pltpu.VMEM((m, 2 * S5_STATE), F32)
    y = pl.pallas_call(
        functools.partial(_s5_kernel, batch=b, n_ctx_steps=ncs, n_lat_steps=nls),
        out_shape=jax.ShapeDtypeStruct((S5_GROUPS, m, width), BF16),
        grid=(S5_GROUPS,),
        in_specs=[
            pl.BlockSpec((1, m, width), lambda g: (g, 0, 0)),
            pl.BlockSpec((1, S5_GW, 2 * width + 128), lambda g: (g, 0, 0)),
            pl.BlockSpec((1, 4, CHUNK, 4 * S5_STATE), lambda g: (g, 0, 0, 0)),
            pl.BlockSpec((1, 4, S5_GW, 4 * S5_STATE), lambda g: (g, 0, 0, 0)),
            pl.BlockSpec((1, 2, 2 * S5_STATE), lambda g: (g, 0, 0)),
            pl.BlockSpec((1, 1, width), lambda g: (g, 0, 0)),
        ],
        out_specs=pl.BlockSpec((1, m, width), lambda g: (g, 0, 0)),
        scratch_shapes=[
            pltpu.VMEM((width, width), BF16),
            pltpu.VMEM((8, S5_GW, 2 * width), BF16),
            pltpu.VMEM((width, 4 * S5_STATE), BF16),
            pltpu.VMEM((width, 4 * S5_STATE), BF16),
            state_rows, state_rows, state_rows, state_rows, state_rows, state_rows,
        ],
        compiler_params=_cparams("arbitrary"),
        name="s5_scan",
    )(ug, strip, rows, cols, decay, dskip)
    return y.reshape(S5_GROUPS, m, CHUNK, S5_GW).transpose(1, 2, 0, 3).reshape(m * CHUNK, D_BRANCH)


def _s5_readout_kernel(y_ref, w_ref, o_ref):
    r = jnp.dot(y_ref[...], w_ref[...], preferred_element_type=F32)
    o_ref[...] = (r[:, :D_BRANCH] * _sigmoid(r[:, D_BRANCH:])).astype(o_ref.dtype)


def _s5_readout(y, w_glu, layer, lay):
    n_tok = y.shape[0]
    rows = lay["tm"]
    return pl.pallas_call(
        _s5_readout_kernel,
        out_shape=jax.ShapeDtypeStruct((n_tok, D_BRANCH), BF16),
        grid=(n_tok // rows,),
        in_specs=[
            pl.BlockSpec((rows, D_BRANCH), lambda i: (i, 0)),
            pl.BlockSpec((None, D_BRANCH, 2 * D_BRANCH), lambda i: (layer, 0, 0)),
        ],
        out_specs=pl.BlockSpec((rows, D_BRANCH), lambda i: (i, 0)),
        compiler_params=_cparams("arbitrary"),
        name="s5_readout",
    )(y, w_glu)


def _merge_kernel(h_ref, ba_ref, bb_ref, bc_ref, bd_ref, wg_ref, bg_ref, wb_ref, o_ref):
    h = h_ref[...]
    acc = None
    for k, br_ref in enumerate((ba_ref, bb_ref, bc_ref, bd_ref)):
        gate = _sigmoid(jnp.dot(h, wg_ref[k], preferred_element_type=F32) + bg_ref[k])
        term = gate * jnp.dot(br_ref[...], wb_ref[k], preferred_element_type=F32)
        acc = term if acc is None else acc + term
    o_ref[...] = acc.astype(o_ref.dtype)


def _merge(h, branches, w_gate, b_gate, w_branch, layer, first, lay):
    n_tok, d = h.shape
    tm, tn = lay["tm"], 512
    br_spec = pl.BlockSpec((tm, D_BRANCH), lambda i, j: (i + first, 0))
    return pl.pallas_call(
        _merge_kernel,
        out_shape=jax.ShapeDtypeStruct((n_tok, d), BF16),
        grid=(n_tok // tm - first, d // tn),
        in_specs=[
            pl.BlockSpec((tm, d), lambda i, j: (i + first, 0)),
            br_spec, br_spec, br_spec, br_spec,
            pl.BlockSpec((None, N_BRANCH, d, tn), lambda i, j: (layer, 0, 0, j)),
            pl.BlockSpec((N_BRANCH, 1, tn), lambda i, j: (0, 0, j)),
            pl.BlockSpec((None, N_BRANCH, D_BRANCH, tn), lambda i, j: (layer, 0, 0, j)),
        ],
        out_specs=pl.BlockSpec((tm, tn), lambda i, j: (i + first, j)),
        compiler_params=_cparams("arbitrary", "arbitrary"),
        name="branch_merge",
    )(h, *branches, w_gate, b_gate.reshape(N_BRANCH, 1, d), w_branch)


def _outproj_kernel(m_ref, w_ref, x_ref, gate_ref, o_ref):
    o_ref[...] = x_ref[...] + gate_ref[0] * jnp.dot(m_ref[...], w_ref[...], preferred_element_type=F32)


def _out_projection(merged, w_out, xall, mod3, layer, first, lay):
    n_tok, d = xall.shape
    split = 2
    tm = lay["tm"] // split
    mod_row = lay["mod_row"]
    skip = first * split
    return pl.pallas_call(
        _outproj_kernel,
        out_shape=jax.ShapeDtypeStruct((n_tok, d), F32),
        grid=(n_tok // tm - skip,),
        in_specs=[
            pl.BlockSpec((tm, d), lambda i: (i + skip, 0)),
            pl.BlockSpec((None, d, d), lambda i: (layer, 0, 0)),
            pl.BlockSpec((tm, d), lambda i: (i + skip, 0)),
            pl.BlockSpec((1, 1, d), lambda i: (mod_row((i + skip) // split, 2), 0, 0)),
        ],
        out_specs=pl.BlockSpec((tm, d), lambda i: (i + skip, 0)),
        compiler_params=_cparams("arbitrary"),
        name="out_projection",
    )(merged, w_out, xall, mod3)


def _ffn_kernel(x_ref, g_ref, shift_ref, scale_ref, gate_ref, w1_ref, w2_ref, o_ref, h_ref):
    j = pl.program_id(1)

    @pl.when(j == 0)
    def _():
        _modulated_rms_norm(x_ref, g_ref, shift_ref, scale_ref, h_ref)
        o_ref[...] = jnp.zeros_like(o_ref)

    a = jnp.maximum(jnp.dot(h_ref[...], w1_ref[...], preferred_element_type=F32), 0.0)
    o_ref[...] += jnp.dot((a * a).astype(BF16), w2_ref[...], preferred_element_type=F32)

    @pl.when(j == pl.num_programs(1) - 1)
    def _():
        o_ref[...] = x_ref[...] + gate_ref[0] * o_ref[...]


def _ffn(xall, norm_g, mod3, w1, w2, layer, first, lay):
    n_tok, d = xall.shape
    tm, tf = lay["tm"], 512
    mod_row = lay["mod_row"]
    mspec = lambda which: pl.BlockSpec((1, 1, d), lambda i, j: (mod_row(i + first, which), 0, 0))
    return pl.pallas_call(
        _ffn_kernel,
        out_shape=jax.ShapeDtypeStruct((n_tok, d), F32),
        grid=(n_tok // tm - first, D_FF // tf),
        in_specs=[
            pl.BlockSpec((tm, d), lambda i, j: (i + first, 0)),
            pl.BlockSpec((1, d), lambda i, j: (0, 0)),
            mspec(3), mspec(4), mspec(5),
            pl.BlockSpec((None, d, tf), lambda i, j: (layer, 0, j)),
            pl.BlockSpec((None, tf, d), lambda i, j: (layer, j, 0)),
        ],
        out_specs=pl.BlockSpec((tm, d), lambda i, j: (i + first, 0)),
        scratch_shapes=[pltpu.VMEM((tm, d), BF16)],
        compiler_params=_cparams("arbitrary", "arbitrary"),
        name="ffn",
    )(xall, norm_g.reshape(1, d), mod3, mod3, mod3, w1, w2)


def _final_norm_kernel(x_ref, g_ref, o_ref):
    x = x_ref[...]
    o_ref[...] = x * lax.rsqrt(jnp.mean(x * x, axis=-1, keepdims=True) + EPS) * g_ref[...]


def _final_norm(xall, g, lay):
    d = xall.shape[1]
    tm = lay["tm"]
    first = lay["nc"] // tm
    n_lat = lay["b"] * lay["s"]
    return pl.pallas_call(
        _final_norm_kernel,
        out_shape=jax.ShapeDtypeStruct((n_lat, d), F32),
        grid=(n_lat // tm,),
        in_specs=[pl.BlockSpec((tm, d), lambda i: (i + first, 0)), pl.BlockSpec((1, d), lambda i: (0, 0))],
        out_specs=pl.BlockSpec((tm, d), lambda i: (i, 0)),
        compiler_params=_cparams("arbitrary"),
        name="final_norm",
    )(xall, g.reshape(1, d))


def _reorder_in_columns(w_in):
    q_end = 5 * D_BRANCH
    kv = 2 * N_KV_HEADS * HEAD_DIM
    pad = jnp.zeros(w_in.shape[:-1] + (IN_COLS_PAD - IN_COLS,), w_in.dtype)
    return jnp.concatenate([w_in[..., :q_end], w_in[..., q_end + kv:], w_in[..., q_end:q_end + kv], pad], axis=-1)


def kernel(x, c, ctx, c_ctx, w_mod, b_mod, norm1_g, norm2_g, w_in, gmlp_ln_g, gmlp_ln_b, gmlp_ws, gmlp_bs, conv_w, conv_b, conv_ln_g, conv_ln_b, attn_sink, s5_a_re, s5_a_im, s5_log_step, s5_b_re, s5_b_im, s5_c_re, s5_c_im, s5_d, s5_w_glu, w_branch, w_gate, b_gate, w_out, w_ff1, w_ff2, final_g):
    b, s, d = x.shape
    n_ctx = ctx.shape[1]
    depth = w_mod.shape[0]
    assert d == D_MODEL and s % GRID_W == 0
    nc = b * n_ctx
    tm = _row_tile(nc, s)
    tseq = min(256, tm)
    nct, lpt = nc // tm, s // tm

    def mod_row(i, which):
        bidx = jnp.where(i < nct, b, (i - nct) // lpt)
        return bidx * N_MOD + which

    lay = dict(b=b, s=s, n_ctx=n_ctx, nc=nc, tm=tm, tseq=tseq, mod_row=mod_row)

    cond = jnp.concatenate([c, c_ctx[None, :]], axis=0)
    cond = jnp.pad(cond, ((0, -cond.shape[0] % 8), (0, 0)))
    mod_all = _modulation(cond, w_mod, b_mod)
    rope_tab = _rope_tables(s)
    xall = jnp.concatenate([ctx.reshape(nc, d), x.reshape(b * s, d)], axis=0)

    w_in_b = _reorder_in_columns(w_in).astype(BF16)
    w_gate_b, w_branch_b, w_out_b = w_gate.astype(BF16), w_branch.astype(BF16), w_out.astype(BF16)
    w_ff1_b, w_ff2_b, w_glu_b = w_ff1.astype(BF16), w_ff2.astype(BF16), s5_w_glu.astype(BF16)

    for l in range(depth):
        first = nct if l == depth - 1 else 0
        mod3 = mod_all[l, :b + 1].reshape((b + 1) * N_MOD, 1, d)
        h, z, ud = _in_projection(xall, norm1_g[l], mod3, w_in_b, l, lay)
        br_a = _gmlp(z, gmlp_ln_g[l], gmlp_ln_b[l], gmlp_ws[l], gmlp_bs[l], lay)
        br_b = _conformer_conv(z, conv_w[l], conv_b[l], conv_ln_g[l], conv_ln_b[l], lay)
        br_c = _attention(z, attn_sink[l], rope_tab, lay)
        tables = _s5_tables(s5_a_re[l], s5_a_im[l], s5_log_step[l], s5_b_re[l], s5_b_im[l], s5_c_re[l], s5_c_im[l])
        y = _s5_mix(ud, tables, s5_d[l], lay)
        br_d = _s5_readout(y, w_glu_b, l, lay)
        merged = _merge(h, (br_a, br_b, br_c, br_d), w_gate_b, b_gate[l], w_branch_b, l, first, lay)
        xall = _out_projection(merged, w_out_b, xall, mod3, l, first, lay)
        xall = _ffn(xall, norm2_g[l], mod3, w_ff1_b, w_ff2_b, l, first, lay)

    return _final_norm(xall, final_g, lay).reshape(b, s, d)
```
